```python
import jax
import jax.numpy as jnp
from jax import lax
import numpy as np

D_MODEL = 1024
BATCH = 16
SEQ = 2048
DEPTH = 2

HEAD_DIM = 64
DIL_PATTERNS = ((128, 1), (512, 4), (2048, 16))
DIL_HEADS_PER_GROUP = 4
DIL_HEADS = len(DIL_PATTERNS) * DIL_HEADS_PER_GROUP
DIL_QBLOCK = 64
NA_HEADS = 8
GRID_W = 64
NA_ROWS_MAX = 8
NA_COLS = 16
NA_QCOLS = 16
D_FF = 2816
ROPE_THETA = 10000.0
RMS_EPS = 1e-6
NEG_INF = -1e30
DIL_QKV = 3 * DIL_HEADS * HEAD_DIM
NA_QKV = 3 * NA_HEADS * HEAD_DIM
IN_WIDTH = DIL_QKV + NA_QKV + 2 * D_MODEL

kernel_name = 'hybrid_dilated_neighbourhood_macaron'


def rms_norm(x, g):
    x32 = x.astype(jnp.float32)
    y = x32 * lax.rsqrt(jnp.mean(x32 * x32, axis=-1, keepdims=True) + RMS_EPS)
    return (y * g.astype(jnp.float32)).astype(x.dtype)


def swiglu(x, w_up, w_down):
    gate, up = jnp.split(x @ w_up, 2, axis=-1)
    return (jax.nn.silu(gate) * up) @ w_down


def rotary(t, pos):
    half = HEAD_DIM // 2
    inv_freq = ROPE_THETA ** (-jnp.arange(half, dtype=jnp.float32) / half)
    ang = pos.astype(jnp.float32)[:, None] * inv_freq[None, :]
    cos = jnp.cos(ang).astype(t.dtype)
    sin = jnp.sin(ang).astype(t.dtype)
    t1, t2 = t[..., :half], t[..., half:]
    return jnp.concatenate([t1 * cos - t2 * sin, t2 * cos + t1 * sin], axis=-1)


def dilated_window_attention(q, k, v, dilation, half):
    b, g, s, hd = q.shape
    sub_len = s // dilation
    bq = min(DIL_QBLOCK, sub_len)
    nb = -(-sub_len // bq)
    padded = nb * bq
    nk = bq + 2 * half

    def to_sub(t):
        return t.reshape(b, g, sub_len, dilation, hd).transpose(0, 1, 3, 2, 4)

    qs = jnp.pad(to_sub(q), ((0, 0), (0, 0), (0, 0), (0, padded - sub_len), (0, 0)))
    qs = qs.reshape(b, g, dilation, nb, bq, hd)
    key_idx = np.arange(nb)[:, None] * bq + np.arange(nk)[None, :]
    kpad = ((0, 0), (0, 0), (0, 0), (half, padded - sub_len + half), (0, 0))
    ks = jnp.pad(to_sub(k), kpad)[:, :, :, key_idx]
    vs = jnp.pad(to_sub(v), kpad)[:, :, :, key_idx]
    q_pos = np.arange(nb)[:, None] * bq + np.arange(bq)[None, :]
    k_pos = key_idx - half
    rel = k_pos[:, None, :] - q_pos[:, :, None]
    valid = (np.abs(rel) <= half) & (k_pos[:, None, :] >= 0) & (k_pos[:, None, :] < sub_len)
    scores = jnp.einsum('bgrnqd,bgrnkd->bgrnqk', qs, ks).astype(jnp.float32) * (hd ** -0.5)
    scores = jnp.where(valid, scores, NEG_INF)
    m = jnp.max(scores, axis=-1, keepdims=True)
    p = jnp.exp(scores - m)
    den = jnp.sum(p, axis=-1)
    o = jnp.einsum('bgrnqk,bgrnkd->bgrnqd', p.astype(v.dtype), vs).astype(jnp.float32) / den[..., None]
    lse = m[..., 0] + jnp.log(den)
    o = o.reshape(b, g, dilation, padded, hd)[:, :, :, :sub_len]
    o = o.transpose(0, 1, 3, 2, 4).reshape(b, g, s, hd)
    lse = lse.reshape(b, g, dilation, padded)[..., :sub_len].transpose(0, 1, 3, 2).reshape(b, g, s)
    return o, lse


def neighbourhood_attention(q, k, v, rel_bias):
    b, nh, s, hd = q.shape
    rows = s // GRID_W
    kr = min(NA_ROWS_MAX, rows)
    n_cb = GRID_W // NA_QCOLS
    kcw = 2 * NA_QCOLS
    q_cols = np.arange(GRID_W).reshape(n_cb, NA_QCOLS)
    key_cols = (np.clip(np.arange(n_cb) * NA_QCOLS - NA_QCOLS // 2, 0, GRID_W - kcw)[:, None]
                + np.arange(kcw)[None, :])
    win_lo = np.clip(q_cols - NA_COLS // 2, 0, GRID_W - NA_COLS)
    col_valid = ((key_cols[:, None, :] >= win_lo[:, :, None])
                 & (key_cols[:, None, :] < win_lo[:, :, None] + NA_COLS))
    col_idx = np.clip(key_cols[:, None, :] - q_cols[:, :, None] + NA_COLS - 1, 0, 2 * NA_COLS - 2)
    row_ids = np.arange(rows)
    row_lo = np.clip(row_ids - kr // 2, 0, rows - kr)
    row_idx = row_lo[:, None] + np.arange(kr)[None, :] - row_ids[:, None] + NA_ROWS_MAX - 1
    bias = rel_bias.astype(jnp.float32)[:, row_idx][..., col_idx]
    bias = jnp.where(col_valid[:, :, None, :], bias.transpose(1, 0, 3, 4, 2, 5), NEG_INF)
    kg = k.reshape(b, nh, rows, GRID_W, hd)
    vg = v.reshape(b, nh, rows, GRID_W, hd)
    q_rows = q.reshape(b, nh, rows, n_cb, NA_QCOLS, hd).transpose(2, 0, 1, 3, 4, 5)
    scale = hd ** -0.5

    def one_row(args):
        q_r, lo, bias_r = args
        k_r = lax.dynamic_slice_in_dim(kg, lo, kr, axis=2)[:, :, :, key_cols]
        v_r = lax.dynamic_slice_in_dim(vg, lo, kr, axis=2)[:, :, :, key_cols]
        sc = jnp.einsum('bhcqd,bhrckd->bhcqrk', q_r, k_r).astype(jnp.float32) * scale + bias_r
        p = jax.nn.softmax(sc.reshape(b, nh, n_cb, NA_QCOLS, kr * kcw), axis=-1).reshape(sc.shape)
        return jnp.einsum('bhcqrk,bhrckd->bhcqd', p.astype(v.dtype), v_r)

    out = lax.map(one_row, (q_rows, jnp.asarray(row_lo, dtype=jnp.int32), bias))
    return out.reshape(rows, b, nh, GRID_W, hd).transpose(1, 2, 0, 3, 4).reshape(b, nh, s, hd)


def hybrid_mixer(h, w_in, rel_bias, w_branch_a, w_branch_b, w_out, pos):
    b, s, _ = h.shape
    proj = h @ w_in
    a_qkv = proj[..., :DIL_QKV]
    b_qkv = proj[..., DIL_QKV:DIL_QKV + NA_QKV]
    gate_a, gate_b = jnp.split(jax.nn.sigmoid(proj[..., DIL_QKV + NA_QKV:]), 2, axis=-1)

    def heads(t, n):
        return t.reshape(b, s, n, HEAD_DIM).transpose(0, 2, 1, 3)

    qa, ka, va = (heads(t, DIL_HEADS) for t in jnp.split(a_qkv, 3, axis=-1))
    qa, ka = rotary(qa, pos), rotary(ka, pos)
    outs, lses = [], []
    for gi, (window, dilation) in enumerate(DIL_PATTERNS):
        grp = slice(gi * DIL_HEADS_PER_GROUP, (gi + 1) * DIL_HEADS_PER_GROUP)
        o, lse = dilated_window_attention(qa[:, grp], ka[:, grp], va[:, grp], dilation, (window // 2) // dilation)
        outs.append(o)
        lses.append(lse)
    mix_w = jax.nn.softmax(jnp.stack(lses), axis=0)
    ya = jnp.sum(mix_w[..., None] * jnp.stack(outs), axis=0).astype(h.dtype)
    ya = ya.transpose(0, 2, 1, 3).reshape(b, s, DIL_HEADS_PER_GROUP * HEAD_DIM)

    qb, kb, vb = (heads(t, NA_HEADS) for t in jnp.split(b_qkv, 3, axis=-1))
    yb = neighbourhood_attention(qb, kb, vb, rel_bias)
    yb = yb.transpose(0, 2, 1, 3).reshape(b, s, NA_HEADS * HEAD_DIM)

    merged = gate_a * (ya @ w_branch_a) + gate_b * (yb @ w_branch_b)
    return merged @ w_out


def setup_inputs(seed: int = 0) -> dict:
    key = jax.random.key(seed)
    ks = jax.random.split(key, 14)
    f32 = jnp.float32

    def normal(k, shape, scale):
        return jax.random.normal(k, shape, f32) * scale

    def gain(k, shape):
        return 1.0 + 0.05 * jax.random.normal(k, shape, f32)

    return {
        'x': normal(ks[0], (BATCH, SEQ, D_MODEL), 1.0),
        'ffn1_norm': gain(ks[1], (DEPTH, D_MODEL)),
        'ffn1_w_up': normal(ks[2], (DEPTH, D_MODEL, 2 * D_FF), D_MODEL ** -0.5),
        'ffn1_w_down': normal(ks[3], (DEPTH, D_FF, D_MODEL), D_FF ** -0.5),
        'mix_norm': gain(ks[4], (DEPTH, D_MODEL)),
        'w_in': normal(ks[5], (DEPTH, D_MODEL, IN_WIDTH), D_MODEL ** -0.5),
        'na_rel_bias': normal(ks[6], (DEPTH, NA_HEADS, 2 * NA_ROWS_MAX - 1, 2 * NA_COLS - 1), 0.1),
        'w_branch_a': normal(ks[7], (DEPTH, DIL_HEADS_PER_GROUP * HEAD_DIM, D_MODEL), (DIL_HEADS_PER_GROUP * HEAD_DIM) ** -0.5),
        'w_branch_b': normal(ks[8], (DEPTH, NA_HEADS * HEAD_DIM, D_MODEL), (NA_HEADS * HEAD_DIM) ** -0.5),
        'w_out': normal(ks[9], (DEPTH, D_MODEL, D_MODEL), D_MODEL ** -0.5),
        'ffn2_norm': gain(ks[10], (DEPTH, D_MODEL)),
        'ffn2_w_up': normal(ks[11], (DEPTH, D_MODEL, 2 * D_FF), D_MODEL ** -0.5),
        'ffn2_w_down': normal(ks[12], (DEPTH, D_FF, D_MODEL), D_FF ** -0.5),
        'final_norm': gain(ks[13], (D_MODEL,)),
    }


def reference(x, ffn1_norm, ffn1_w_up, ffn1_w_down, mix_norm, w_in, na_rel_bias, w_branch_a,
              w_branch_b, w_out, ffn2_norm, ffn2_w_up, ffn2_w_down, final_norm):
    pos = jnp.arange(x.shape[1])
    for l in range(DEPTH):
        x = x + 0.5 * swiglu(rms_norm(x, ffn1_norm[l]), ffn1_w_up[l], ffn1_w_down[l])
        x = x + hybrid_mixer(rms_norm(x, mix_norm[l]), w_in[l], na_rel_bias[l], w_branch_a[l],
                             w_branch_b[l], w_out[l], pos)
        x = x + 0.5 * swiglu(rms_norm(x, ffn2_norm[l]), ffn2_w_up[l], ffn2_w_down[l])
    return rms_norm(x, final_norm)
```

```python
import functools

import numpy as np
import jax
import jax.numpy as jnp
from jax import lax
from jax.experimental import pallas as pl
from jax.experimental.pallas import tpu as pltpu

F32 = jnp.float32
BF16 = jnp.bfloat16

D_MODEL = 1024
SEQ = 2048
DEPTH = 2
HEAD_DIM = 64
DILATIONS = (1, 4, 16)
BAND = 64
GROUP_HEADS = 4
GROUP_W = GROUP_HEADS * HEAD_DIM
NA_HEADS = 8
NA_W = NA_HEADS * HEAD_DIM
GRID_W = 64
GRID_ROWS = SEQ // GRID_W
NA_ROWS = 8
NA_COLS = 16
D_FF = 2816
ROPE_THETA = 10000.0
RMS_EPS = 1e-6
NEG_INF = -1e30
DIL_QKV = 3 * 3 * GROUP_W
NA_QKV = 3 * NA_W

LANES = 128
VMEM_LIMIT_BYTES = 56 * 1024 * 1024

TM = 512
FF_CHUNK = 1408
NA_QROWS = 2
NA_KROWS = 10
NA_KEYS = NA_KROWS * GRID_W
NA_CASES = 5


def _rms(x, g):
    return x * lax.rsqrt(jnp.mean(x * x, axis=-1, keepdims=True) + RMS_EPS) * g


def _swiglu_half(h, w_up_ref, w_down_ref):
    acc = None
    for c in range(D_FF // FF_CHUNK):
        lo = c * FF_CHUNK
        g = jnp.dot(h, w_up_ref[:, lo:lo + FF_CHUNK], preferred_element_type=F32)
        u = jnp.dot(h, w_up_ref[:, D_FF + lo:D_FF + lo + FF_CHUNK], preferred_element_type=F32)
        a = (g * jax.nn.sigmoid(g) * u).astype(BF16)
        part = jnp.dot(a, w_down_ref[lo:lo + FF_CHUNK, :], preferred_element_type=F32)
        acc = part if acc is None else acc + part
    return acc


def _rope(t, cos, sin_signed):
    lane = lax.broadcasted_iota(jnp.int32, t.shape, 1)
    first_half = (lane % HEAD_DIM) < (HEAD_DIM // 2)
    partner = jnp.where(first_half, pltpu.roll(t, LANES - HEAD_DIM // 2, 1), pltpu.roll(t, HEAD_DIM // 2, 1))
    return t * cos + partner * sin_signed


def _ffn_proj_kernel(x_ref, g1_ref, wup_ref, wdown_ref, g2_ref, wnat_ref, wg1_ref, wg2_ref,
                     cos0_ref, sin0_ref, cos1_ref, sin1_ref, cos2_ref, sin2_ref,
                     x1_ref, a0_ref, a1_ref, a2_ref, nb_ref,
                     slab_ref, hperm_ref):
    x = x_ref[...]
    h = _rms(x, g1_ref[...]).astype(BF16)
    x1 = x + 0.5 * _swiglu_half(h, wup_ref, wdown_ref)
    x1_ref[...] = x1

    h2 = _rms(x1, g2_ref[...])
    n_slabs = D_MODEL // LANES
    for c in range(n_slabs):
        slab_ref[c] = h2[:, c * LANES:(c + 1) * LANES]

    scale = HEAD_DIM ** -0.5

    def write_group(a_ref, p, cos_ref, sin_ref):
        cos = cos_ref[...]
        sin = sin_ref[...]
        for c in range(GROUP_W // LANES):
            q = p[:, c * LANES:(c + 1) * LANES] * scale
            a_ref[:, c * LANES:(c + 1) * LANES] = _rope(q, cos, sin).astype(BF16)
            k = p[:, GROUP_W + c * LANES:GROUP_W + (c + 1) * LANES]
            a_ref[:, GROUP_W + c * LANES:GROUP_W + (c + 1) * LANES] = _rope(k, cos, sin).astype(BF16)
        a_ref[:, 2 * GROUP_W:] = p[:, 2 * GROUP_W:].astype(BF16)

    pn = jnp.dot(h2.astype(BF16), wnat_ref[...], preferred_element_type=F32)
    write_group(a0_ref, pn[:, :3 * GROUP_W], cos0_ref, sin0_ref)
    nb_ref[:, :NA_W] = (pn[:, 3 * GROUP_W:3 * GROUP_W + NA_W] * scale).astype(BF16)
    nb_ref[:, NA_W:] = pn[:, 3 * GROUP_W + NA_W:].astype(BF16)

    for d, w_ref, a_ref, cos_ref, sin_ref in ((DILATIONS[1], wg1_ref, a1_ref, cos1_ref, sin1_ref),
                                              (DILATIONS[2], wg2_ref, a2_ref, cos2_ref, sin2_ref)):
        n = TM // d
        for r in range(d):
            for c in range(n_slabs):
                hperm_ref[r * n:(r + 1) * n, c * LANES:(c + 1) * LANES] = (
                    slab_ref[c, pl.ds(r, n, stride=d), :].astype(BF16))
        pg = jnp.dot(hperm_ref[...], w_ref[...], preferred_element_type=F32)
        write_group(a_ref, pg, cos_ref, sin_ref)


def _resident(shape):
    nd = len(shape)
    return pl.BlockSpec(shape, lambda *_: (0,) * nd, pipeline_mode=pl.Buffered(1))


def _ffn_proj(x, g1, wup, wdown, g2, wnat, wg1, wg2, tables):
    b = x.shape[0]
    nt = SEQ // TM
    row = lambda w: pl.BlockSpec((None, TM, w), lambda i, j: (i, j, 0))
    tab = pl.BlockSpec((TM, LANES), lambda i, j: (j, 0))
    out_shape = (
        jax.ShapeDtypeStruct((b, SEQ, D_MODEL), F32),
        jax.ShapeDtypeStruct((b, SEQ, 3 * GROUP_W), BF16),
        jax.ShapeDtypeStruct((b, SEQ, 3 * GROUP_W), BF16),
        jax.ShapeDtypeStruct((b, SEQ, 3 * GROUP_W), BF16),
        jax.ShapeDtypeStruct((b, SEQ, NA_QKV), BF16),
    )
    return pl.pallas_call(
        _ffn_proj_kernel,
        grid=(b, nt),
        in_specs=[row(D_MODEL), _resident(g1.shape), _resident(wup.shape), _resident(wdown.shape),
                  _resident(g2.shape), _resident(wnat.shape), _resident(wg1.shape), _resident(wg2.shape),
                  tab, tab, tab, tab, tab, tab],
        out_specs=(row(D_MODEL), row(3 * GROUP_W), row(3 * GROUP_W), row(3 * GROUP_W), row(NA_QKV)),
        out_shape=out_shape,
        scratch_shapes=[pltpu.VMEM((D_MODEL // LANES, TM, LANES), F32), pltpu.VMEM((TM, D_MODEL), BF16)],
        compiler_params=pltpu.CompilerParams(dimension_semantics=("parallel", "parallel"),
                                             vmem_limit_bytes=VMEM_LIMIT_BYTES),
        name="ffn_proj",
    )(x, g1, wup, wdown, g2, wnat, wg1, wg2, *tables)


def _split_heads(q_ref, qm_ref):
    q = q_ref[...].astype(F32)
    lane = lax.broadcasted_iota(jnp.int32, q.shape, 1)
    qm_ref[0] = jnp.where(lane < HEAD_DIM, q, 0.0).astype(BF16)
    qm_ref[1] = jnp.where(lane >= HEAD_DIM, q, 0.0).astype(BF16)


def _pair_attend(q_pair, k, v, bias):
    res = []
    for h in range(2):
        bh = bias[h] if isinstance(bias, (tuple, list)) else bias
        s = lax.dot_general(q_pair[h], k, (((1,), (1,)), ((), ())), preferred_element_type=F32) + bh
        m = jnp.max(s, axis=-1, keepdims=True)
        p = jnp.exp(s - m)
        l = jnp.sum(p, axis=-1, keepdims=True)
        pv = jnp.dot(p.astype(BF16), v, preferred_element_type=F32)
        res.append((pv, m, l))
    shape = res[0][0].shape
    left = lax.broadcasted_iota(jnp.int32, shape, 1) < HEAD_DIM
    acc = jnp.where(left, res[0][0], res[1][0])
    m = jnp.where(left, jnp.broadcast_to(res[0][1], shape), jnp.broadcast_to(res[1][1], shape))
    l = jnp.where(left, jnp.broadcast_to(res[0][2], shape), jnp.broadcast_to(res[1][2], shape))
    return acc, m, l


CHUNK = 2 * BAND


def _band_bias(m, n, lo, hi):
    u = lax.broadcasted_iota(jnp.int32, (m, n), 0)
    w = lax.broadcasted_iota(jnp.int32, (m, n), 1)
    d = w - u
    return jnp.where(d >= lo, jnp.where(d <= hi, 0.0, NEG_INF), NEG_INF).astype(F32)


def _dilated_kernel(q0_ref, k0_ref, v0_ref, q1_ref, k1_ref, v1_ref, q2_ref, k2_ref, v2_ref,
                    o_ref, qm_ref, acc_ref, m_ref, l_ref):
    for g, q_ref in enumerate((q0_ref, q1_ref, q2_ref)):
        _split_heads(q_ref, qm_ref.at[g])

    bias_first = _band_bias(BAND, CHUNK, -BAND, BAND)
    bias_last = _band_bias(BAND, CHUNK, 0, CHUNK)
    bias_full = _band_bias(CHUNK, 2 * CHUNK, 0, CHUNK)
    bias_one = _band_bias(CHUNK, CHUNK, -BAND, BAND)

    def put(g, dst, n, stride, vals):
        acc, m, l = vals
        if stride == 1:
            idx = pl.ds(dst, n)
        else:
            idx = pl.ds(dst, n, stride=stride)
        acc_ref[g, idx, :] = acc
        m_ref[g, idx, :] = m
        l_ref[g, idx, :] = l

    nc0 = SEQ // CHUNK
    qs = [qm_ref[0, h, 0:BAND, :] for h in range(2)]
    put(0, 0, BAND, 1, _pair_attend(qs, k0_ref[0:CHUNK, :], v0_ref[0:CHUNK, :], bias_first))

    def body0(i, carry):
        k_off = pl.multiple_of(i * CHUNK, CHUNK)
        q_off = pl.multiple_of(i * CHUNK + BAND, BAND)
        qs = [qm_ref[0, h, pl.ds(q_off, CHUNK), :] for h in range(2)]
        vals = _pair_attend(qs, k0_ref[pl.ds(k_off, 2 * CHUNK), :], v0_ref[pl.ds(k_off, 2 * CHUNK), :], bias_full)
        put(0, q_off, CHUNK, 1, vals)
        return carry

    lax.fori_loop(0, nc0 - 1, body0, 0)
    last = SEQ - CHUNK
    qs = [qm_ref[0, h, last + BAND:SEQ, :] for h in range(2)]
    put(0, last + BAND, BAND, 1, _pair_attend(qs, k0_ref[last:SEQ, :], v0_ref[last:SEQ, :], bias_last))

    d1 = DILATIONS[1]
    nt = SEQ // TM
    for r in range(d1):
        base = [j * TM + r * CHUNK for j in range(nt)]
        nat = [j * TM + r for j in range(nt)]
        qs = [qm_ref[1, h, base[0]:base[0] + BAND, :] for h in range(2)]
        vals = _pair_attend(qs, k1_ref[base[0]:base[0] + CHUNK, :], v1_ref[base[0]:base[0] + CHUNK, :], bias_first)
        put(1, nat[0], BAND, d1, vals)
        for j in range(nt - 1):
            qs = [jnp.concatenate([qm_ref[1, h, base[j] + BAND:base[j] + CHUNK, :],
                                   qm_ref[1, h, base[j + 1]:base[j + 1] + BAND, :]], axis=0) for h in range(2)]
            k = jnp.concatenate([k1_ref[base[j]:base[j] + CHUNK, :], k1_ref[base[j + 1]:base[j + 1] + CHUNK, :]], axis=0)
            v = jnp.concatenate([v1_ref[base[j]:base[j] + CHUNK, :], v1_ref[base[j + 1]:base[j + 1] + CHUNK, :]], axis=0)
            acc, m, l = _pair_attend(qs, k, v, bias_full)
            put(1, nat[j] + BAND * d1, BAND, d1, (acc[:BAND], m[:BAND], l[:BAND]))
            put(1, nat[j + 1], BAND, d1, (acc[BAND:], m[BAND:], l[BAND:]))
        qs = [qm_ref[1, h, base[-1] + BAND:base[-1] + CHUNK, :] for h in range(2)]
        vals = _pair_attend(qs, k1_ref[base[-1]:base[-1] + CHUNK, :], v1_ref[base[-1]:base[-1] + CHUNK, :], bias_last)
        put(1, nat[-1] + BAND * d1, BAND, d1, vals)

    d2 = DILATIONS[2]
    piece = TM // d2

    def body2(r, carry):
        offs = [pl.multiple_of(j * TM + r * piece, piece) for j in range(nt)]
        qs = [jnp.concatenate([qm_ref[2, h, pl.ds(o, piece), :] for o in offs], axis=0) for h in range(2)]
        k = jnp.concatenate([k2_ref[pl.ds(o, piece), :] for o in offs], axis=0)
        v = jnp.concatenate([v2_ref[pl.ds(o, piece), :] for o in offs], axis=0)
        acc, m, l = _pair_attend(qs, k, v, bias_one)
        for j in range(nt):
            sl = slice(j * piece, (j + 1) * piece)
            put(2, j * TM + r, piece, d2, (acc[sl], m[sl], l[sl]))
        return carry

    lax.fori_loop(0, d2, body2, 0)

    rows = 256
    for c in range(SEQ // rows):
        sl = slice(c * rows, (c + 1) * rows)
        ms = [m_ref[g, sl, :] for g in range(3)]
        mx = jnp.maximum(jnp.maximum(ms[0], ms[1]), ms[2])
        num = None
        den = None
        for g in range(3):
            w = jnp.exp(ms[g] - mx)
            n_g = w * acc_ref[g, sl, :]
            d_g = w * l_ref[g, sl, :]
            num = n_g if num is None else num + n_g
            den = d_g if den is None else den + d_g
        o_ref[sl, :] = (num / den).astype(BF16)


def _dilated(a0, a1, a2):
    b = a0.shape[0]
    n_pairs = GROUP_W // LANES

    def col(off):
        return pl.BlockSpec((None, SEQ, LANES), lambda i, p: (i, 0, off + p))

    qkv = [col(0), col(n_pairs), col(2 * n_pairs)]
    return pl.pallas_call(
        _dilated_kernel,
        grid=(b, n_pairs),
        in_specs=qkv * 3,
        out_specs=pl.BlockSpec((None, SEQ, LANES), lambda i, p: (i, 0, p)),
        out_shape=jax.ShapeDtypeStruct((b, SEQ, GROUP_W), BF16),
        scratch_shapes=[pltpu.VMEM((3, 2, SEQ, LANES), BF16),
                        pltpu.VMEM((3, SEQ, LANES), F32),
                        pltpu.VMEM((3, SEQ, LANES), F32),
                        pltpu.VMEM((3, SEQ, LANES), F32)],
        compiler_params=pltpu.CompilerParams(dimension_semantics=("parallel", "parallel"),
                                             vmem_limit_bytes=VMEM_LIMIT_BYTES),
        name="dilated",
    )(a0, a0, a0, a1, a1, a1, a2, a2, a2)


def _na_key_row_start(a):
    return jnp.clip(NA_QROWS * a - NA_ROWS // 2, 0, GRID_ROWS - NA_KROWS)


def _na_case(a):
    n_blocks = GRID_ROWS // NA_QROWS
    return jnp.where(a < 2, a, jnp.where(a > n_blocks - 3, a - (n_blocks - NA_CASES), 2))


def _na_kernel(q_ref, k_ref, v_ref, bias_ref, o_ref, qm_ref):
    _split_heads(q_ref, qm_ref)
    qn = NA_QROWS * GRID_W

    def body(a, carry):
        k_off = pl.multiple_of(_na_key_row_start(a) * GRID_W, LANES)
        q_off = pl.multiple_of(a * qn, qn)
        case = _na_case(a)
        qs = [qm_ref[h, pl.ds(q_off, qn), :] for h in range(2)]
        bias = [bias_ref[h, case] for h in range(2)]
        acc, _, l = _pair_attend(qs, k_ref[pl.ds(k_off, NA_KEYS), :], v_ref[pl.ds(k_off, NA_KEYS), :], bias)
        o_ref[pl.ds(q_off, qn), :] = (acc / l).astype(BF16)
        return carry

    lax.fori_loop(0, GRID_ROWS // NA_QROWS, body, 0)


def _nbr(nb, bias_tab):
    b = nb.shape[0]
    n_pairs = NA_W // LANES

    def col(off):
        return pl.BlockSpec((None, SEQ, LANES), lambda p, i: (i, 0, off + p))

    return pl.pallas_call(
        _na_kernel,
        grid=(n_pairs, b),
        in_specs=[col(0), col(n_pairs), col(2 * n_pairs),
                  pl.BlockSpec((None, 2, NA_CASES, NA_QROWS * GRID_W, NA_KEYS), lambda p, i: (p, 0, 0, 0, 0))],
        out_specs=pl.BlockSpec((None, SEQ, LANES), lambda p, i: (i, 0, p)),
        out_shape=jax.ShapeDtypeStruct((b, SEQ, NA_W), BF16),
        scratch_shapes=[pltpu.VMEM((2, SEQ, LANES), BF16)],
        compiler_params=pltpu.CompilerParams(dimension_semantics=("parallel", "parallel"),
                                             vmem_limit_bytes=VMEM_LIMIT_BYTES),
        name="nbr",
    )(nb, nb, nb, bias_tab)


def _na_bias_table(rel_bias):
    n_blocks = GRID_ROWS // NA_QROWS
    case_block = [0, 1, 2, n_blocks - 2, n_blocks - 1]
    u = np.arange(NA_QROWS * GRID_W)
    w = np.arange(NA_KEYS)
    row_i, col_i, ok = [], [], []
    for a in case_block:
        start = int(np.clip(NA_QROWS * a - NA_ROWS // 2, 0, GRID_ROWS - NA_KROWS))
        qr = NA_QROWS * a + u // GRID_W
        qc = u % GRID_W
        kr = start + w // GRID_W
        kc = w % GRID_W
        row_lo = np.clip(qr - NA_ROWS // 2, 0, GRID_ROWS - NA_ROWS)
        win_lo = np.clip(qc - NA_COLS // 2, 0, GRID_W - NA_COLS)
        valid = ((kr[None, :] >= row_lo[:, None]) & (kr[None, :] < row_lo[:, None] + NA_ROWS)
                 & (kc[None, :] >= win_lo[:, None]) & (kc[None, :] < win_lo[:, None] + NA_COLS))
        row_i.append(np.clip(kr[None, :] - qr[:, None] + NA_ROWS - 1, 0, 2 * NA_ROWS - 2))
        col_i.append(np.clip(kc[None, :] - qc[:, None] + NA_COLS - 1, 0, 2 * NA_COLS - 2))
        ok.append(valid)
    row_i, col_i, ok = np.stack(row_i), np.stack(col_i), np.stack(ok)
    tab = rel_bias.astype(F32)[:, row_i, col_i]
    tab = jnp.where(ok[None], tab, NEG_INF)
    return tab.reshape(NA_HEADS // 2, 2, NA_CASES, NA_QROWS * GRID_W, NA_KEYS)


def _out_ffn_kernel(x1_ref, ya_ref, yb_ref, gm_ref, wgate_ref, wa_ref, wb_ref, wout_ref,
                    g3_ref, wup_ref, wdown_ref, gf_ref, o_ref, *, final):
    x1 = x1_ref[...]
    h = _rms(x1, gm_ref[...]).astype(BF16)
    gates = jax.nn.sigmoid(jnp.dot(h, wgate_ref[...], preferred_element_type=F32))
    ba = jnp.dot(ya_ref[...], wa_ref[...], preferred_element_type=F32)
    bb = jnp.dot(yb_ref[...], wb_ref[...], preferred_element_type=F32)
    merged = gates[:, :D_MODEL] * ba + gates[:, D_MODEL:] * bb
    x2 = x1 + jnp.dot(merged.astype(BF16), wout_ref[...], preferred_element_type=F32)
    h3 = _rms(x2, g3_ref[...]).astype(BF16)
    x3 = x2 + 0.5 * _swiglu_half(h3, wup_ref, wdown_ref)
    if final:
        x3 = _rms(x3, gf_ref[...])
    o_ref[...] = x3


def _out_ffn(x1, ya, yb, gm, wgate, wa, wb, wout, g3, wup, wdown, gf, final):
    b = x1.shape[0]
    row = lambda w: pl.BlockSpec((None, TM, w), lambda i, j: (i, j, 0))
    weights = (gm, wgate, wa, wb, wout, g3, wup, wdown, gf)
    return pl.pallas_call(
        functools.partial(_out_ffn_kernel, final=final),
        grid=(b, SEQ // TM),
        in_specs=[row(D_MODEL), row(GROUP_W), row(NA_W)] + [_resident(w.shape) for w in weights],
        out_specs=row(D_MODEL),
        out_shape=jax.ShapeDtypeStruct((b, SEQ, D_MODEL), F32),
        compiler_params=pltpu.CompilerParams(dimension_semantics=("parallel", "parallel"),
                                             vmem_limit_bytes=VMEM_LIMIT_BYTES),
        name="out_ffn",
    )(x1, ya, yb, *weights)


def _rope_tables():
    half = HEAD_DIM // 2
    pos = jnp.arange(SEQ)
    inv_freq = ROPE_THETA ** (-jnp.arange(half, dtype=F32) / half)
    ang = pos.astype(F32)[:, None] * inv_freq[None, :]
    cos = jnp.cos(ang)
    sin = jnp.sin(ang)
    cos_slab = jnp.tile(cos, (1, LANES // half))
    sin_slab = jnp.tile(jnp.concatenate([-sin, sin], axis=-1), (1, LANES // HEAD_DIM))
    tables = [cos_slab, sin_slab]
    for d in DILATIONS[1:]:
        n = np.arange(SEQ)
        tile, local = n // TM, n % TM
        r, l = local // (TM // d), local % (TM // d)
        token = tile * TM + l * d + r
        tables += [cos_slab[token], sin_slab[token]]
    return tables


def kernel(x, ffn1_norm, ffn1_w_up, ffn1_w_down, mix_norm, w_in, na_rel_bias, w_branch_a, w_branch_b,
           w_out, ffn2_norm, ffn2_w_up, ffn2_w_down, final_norm):
    assert x.shape[1:] == (SEQ, D_MODEL) and x.dtype == F32
    tables = _rope_tables()
    gain = lambda g: g.astype(F32).reshape(1, D_MODEL)
    for layer in range(DEPTH):
        w = w_in[layer].astype(BF16)
        aq, ak, av = (w[:, i * 3 * GROUP_W:(i + 1) * 3 * GROUP_W] for i in range(3))
        grp = lambda t, g: t[:, g * GROUP_W:(g + 1) * GROUP_W]
        wnat = jnp.concatenate([grp(aq, 0), grp(ak, 0), grp(av, 0), w[:, DIL_QKV:DIL_QKV + NA_QKV]], axis=1)
        wg1 = jnp.concatenate([grp(aq, 1), grp(ak, 1), grp(av, 1)], axis=1)
        wg2 = jnp.concatenate([grp(aq, 2), grp(ak, 2), grp(av, 2)], axis=1)
        wgate = w[:, DIL_QKV + NA_QKV:]

        x1, a0, a1, a2, nb = _ffn_proj(
            x, gain(ffn1_norm[layer]), ffn1_w_up[layer].astype(BF16), ffn1_w_down[layer].astype(BF16),
            gain(mix_norm[layer]), wnat, wg1, wg2, tables)
        ya = _dilated(a0, a1, a2)
        yb = _nbr(nb, _na_bias_table(na_rel_bias[layer]))
        x = _out_ffn(x1, ya, yb, gain(mix_norm[layer]), wgate, w_branch_a[layer].astype(BF16),
                     w_branch_b[layer].astype(BF16), w_out[layer].astype(BF16), gain(ffn2_norm[layer]),
                     ffn2_w_up[layer].astype(BF16), ffn2_w_down[layer].astype(BF16), gain(final_norm),
                     final=(layer == DEPTH - 1))
    return x
```

```python
import functools

import numpy as np
import jax
import jax.numpy as jnp
from jax import lax
from jax.experimental import pallas as pl
from jax.experimental.pallas import tpu as pltpu

F32 = jnp.float32
BF16 = jnp.bfloat16

D_MODEL = 1024
SEQ = 2048
DEPTH = 2
HEAD_DIM = 64
DILATIONS = (1, 4, 16)
BAND = 64
GROUP_HEADS = 4
GROUP_W = GROUP_HEADS * HEAD_DIM
NA_HEADS = 8
NA_W = NA_HEADS * HEAD_DIM
GRID_W = 64
GRID_ROWS = SEQ // GRID_W
NA_ROWS = 8
NA_COLS = 16
D_FF = 2816
ROPE_THETA = 10000.0
RMS_EPS = 1e-6
NEG_INF = -1e30
DIL_QKV = 3 * 3 * GROUP_W
NA_QKV = 3 * NA_W

LANES = 128
VMEM_LIMIT_BYTES = 56 * 1024 * 1024

TM = 512
FF_CHUNK = 1408
NA_QROWS = 2
NA_KROWS = 10
NA_KEYS = NA_KROWS * GRID_W
NA_CASES = 5


def _rms(x, g):
    return x * lax.rsqrt(jnp.mean(x * x, axis=-1, keepdims=True) + RMS_EPS) * g


def _swiglu_half(h, w_up_ref, w_down_ref):
    acc = None
    for c in range(D_FF // FF_CHUNK):
        lo = c * FF_CHUNK
        g = jnp.dot(h, w_up_ref[:, lo:lo + FF_CHUNK], preferred_element_type=F32)
        u = jnp.dot(h, w_up_ref[:, D_FF + lo:D_FF + lo + FF_CHUNK], preferred_element_type=F32)
        a = (g * jax.nn.sigmoid(g) * u).astype(BF16)
        part = jnp.dot(a, w_down_ref[lo:lo + FF_CHUNK, :], preferred_element_type=F32)
        acc = part if acc is None else acc + part
    return acc


def _rope(t, cos, sin_signed):
    lane = lax.broadcasted_iota(jnp.int32, t.shape, 1)
    first_half = (lane % HEAD_DIM) < (HEAD_DIM // 2)
    partner = jnp.where(first_half, pltpu.roll(t, LANES - HEAD_DIM // 2, 1), pltpu.roll(t, HEAD_DIM // 2, 1))
    return t * cos + partner * sin_signed


def _ffn_proj_kernel(x_ref, g1_ref, wup_ref, wdown_ref, g2_ref, wnat_ref, wg1_ref, wg2_ref,
                     cos0_ref, sin0_ref, cos1_ref, sin1_ref, cos2_ref, sin2_ref,
                     x1_ref, a0_ref, a1_ref, a2_ref, nb_ref,
                     slab_ref, hperm_ref):
    x = x_ref[...]
    h = _rms(x, g1_ref[...]).astype(BF16)
    x1 = x + 0.5 * _swiglu_half(h, wup_ref, wdown_ref)
    x1_ref[...] = x1

    h2 = _rms(x1, g2_ref[...])
    n_slabs = D_MODEL // LANES
    for c in range(n_slabs):
        slab_ref[c] = h2[:, c * LANES:(c + 1) * LANES]

    scale = HEAD_DIM ** -0.5

    def write_group(a_ref, p, cos_ref, sin_ref):
        cos = cos_ref[...]
        sin = sin_ref[...]
        for c in range(GROUP_W // LANES):
            q = p[:, c * LANES:(c + 1) * LANES] * scale
            a_ref[:, c * LANES:(c + 1) * LANES] = _rope(q, cos, sin).astype(BF16)
            k = p[:, GROUP_W + c * LANES:GROUP_W + (c + 1) * LANES]
            a_ref[:, GROUP_W + c * LANES:GROUP_W + (c + 1) * LANES] = _rope(k, cos, sin).astype(BF16)
        a_ref[:, 2 * GROUP_W:] = p[:, 2 * GROUP_W:].astype(BF16)

    pn = jnp.dot(h2.astype(BF16), wnat_ref[...], preferred_element_type=F32)
    write_group(a0_ref, pn[:, :3 * GROUP_W], cos0_ref, sin0_ref)
    nb_ref[:, :NA_W] = (pn[:, 3 * GROUP_W:3 * GROUP_W + NA_W] * scale).astype(BF16)
    nb_ref[:, NA_W:] = pn[:, 3 * GROUP_W + NA_W:].astype(BF16)

    for d, w_ref, a_ref, cos_ref, sin_ref in ((DILATIONS[1], wg1_ref, a1_ref, cos1_ref, sin1_ref),
                                              (DILATIONS[2], wg2_ref, a2_ref, cos2_ref, sin2_ref)):
        n = TM // d
        for r in range(d):
            for c in range(n_slabs):
                hperm_ref[r * n:(r + 1) * n, c * LANES:(c + 1) * LANES] = (
                    slab_ref[c, pl.ds(r, n, stride=d), :].astype(BF16))
        pg = jnp.dot(hperm_ref[...], w_ref[...], preferred_element_type=F32)
        write_group(a_ref, pg, cos_ref, sin_ref)


def _resident(shape):
    nd = len(shape)
    return pl.BlockSpec(shape, lambda *_: (0,) * nd, pipeline_mode=pl.Buffered(1))


def _ffn_proj(x, g1, wup, wdown, g2, wnat, wg1, wg2, tables):
    b = x.shape[0]
    nt = SEQ // TM
    row = lambda w: pl.BlockSpec((None, TM, w), lambda i, j: (i, j, 0))
    tab = pl.BlockSpec((TM, LANES), lambda i, j: (j, 0))
    out_shape = (
        jax.ShapeDtypeStruct((b, SEQ, D_MODEL), F32),
        jax.ShapeDtypeStruct((b, SEQ, 3 * GROUP_W), BF16),
        jax.ShapeDtypeStruct((b, SEQ, 3 * GROUP_W), BF16),
        jax.ShapeDtypeStruct((b, SEQ, 3 * GROUP_W), BF16),
        jax.ShapeDtypeStruct((b, SEQ, NA_QKV), BF16),
    )
    return pl.pallas_call(
        _ffn_proj_kernel,
        grid=(b, nt),
        in_specs=[row(D_MODEL), _resident(g1.shape), _resident(wup.shape), _resident(wdown.shape),
                  _resident(g2.shape), _resident(wnat.shape), _resident(wg1.shape), _resident(wg2.shape),
                  tab, tab, tab, tab, tab, tab],
        out_specs=(row(D_MODEL), row(3 * GROUP_W), row(3 * GROUP_W), row(3 * GROUP_W), row(NA_QKV)),
        out_shape=out_shape,
        scratch_shapes=[pltpu.VMEM((D_MODEL // LANES, TM, LANES), F32), pltpu.VMEM((TM, D_MODEL), BF16)],
        compiler_params=pltpu.CompilerParams(dimension_semantics=("parallel", "parallel"),
                                             vmem_limit_bytes=VMEM_LIMIT_BYTES),
        name="ffn_proj",
    )(x, g1, wup, wdown, g2, wnat, wg1, wg2, *tables)


def _split_heads(q_ref, qm_ref):
    q = q_ref[...].astype(F32)
    lane = lax.broadcasted_iota(jnp.int32, q.shape, 1)
    qm_ref[0] = jnp.where(lane < HEAD_DIM, q, 0.0).astype(BF16)
    qm_ref[1] = jnp.where(lane >= HEAD_DIM, q, 0.0).astype(BF16)


def _pair_attend(q_pair, k, v, bias):
    res = []
    for h in range(2):
        bh = bias[h] if isinstance(bias, (tuple, list)) else bias
        s = lax.dot_general(q_pair[h], k, (((1,), (1,)), ((), ())), preferred_element_type=F32) + bh
        m = jnp.max(s, axis=-1, keepdims=True)
        p = jnp.exp(s - m)
        l = jnp.sum(p, axis=-1, keepdims=True)
        pv = jnp.dot(p.astype(BF16), v, preferred_element_type=F32)
        res.append((pv, m, l))
    shape = res[0][0].shape
    left = lax.broadcasted_iota(jnp.int32, shape, 1) < HEAD_DIM
    acc = jnp.where(left, res[0][0], res[1][0])
    m = jnp.where(left, jnp.broadcast_to(res[0][1], shape), jnp.broadcast_to(res[1][1], shape))
    l = jnp.where(left, jnp.broadcast_to(res[0][2], shape), jnp.broadcast_to(res[1][2], shape))
    return acc, m, l


CHUNK = 2 * BAND


def _band_bias(m, n, lo, hi):
    u = lax.broadcasted_iota(jnp.int32, (m, n), 0)
    w = lax.broadcasted_iota(jnp.int32, (m, n), 1)
    d = w - u
    return jnp.where(d >= lo, jnp.where(d <= hi, 0.0, NEG_INF), NEG_INF).astype(F32)


def _dilated_kernel(q0_ref, k0_ref, v0_ref, q1_ref, k1_ref, v1_ref, q2_ref, k2_ref, v2_ref,
                    o_ref, qm_ref, acc_ref, m_ref, l_ref):
    for g, q_ref in enumerate((q0_ref, q1_ref, q2_ref)):
        _split_heads(q_ref, qm_ref.at[g])

    bias_first = _band_bias(BAND, CHUNK, -BAND, BAND)
    bias_last = _band_bias(BAND, CHUNK, 0, CHUNK)
    bias_full = _band_bias(CHUNK, 2 * CHUNK, 0, CHUNK)
    bias_one = _band_bias(CHUNK, CHUNK, -BAND, BAND)

    def put(g, dst, n, stride, vals):
        acc, m, l = vals
        if stride == 1:
            idx = pl.ds(dst, n)
        else:
            idx = pl.ds(dst, n, stride=stride)
        acc_ref[g, idx, :] = acc
        m_ref[g, idx, :] = m
        l_ref[g, idx, :] = l

    nc0 = SEQ // CHUNK
    qs = [qm_ref[0, h, 0:BAND, :] for h in range(2)]
    put(0, 0, BAND, 1, _pair_attend(qs, k0_ref[0:CHUNK, :], v0_ref[0:CHUNK, :], bias_first))

    def body0(i, carry):
        k_off = pl.multiple_of(i * CHUNK, CHUNK)
        q_off = pl.multiple_of(i * CHUNK + BAND, BAND)
        qs = [qm_ref[0, h, pl.ds(q_off, CHUNK), :] for h in range(2)]
        vals = _pair_attend(qs, k0_ref[pl.ds(k_off, 2 * CHUNK), :], v0_ref[pl.ds(k_off, 2 * CHUNK), :], bias_full)
        put(0, q_off, CHUNK, 1, vals)
        return carry

    lax.fori_loop(0, nc0 - 1, body0, 0)
    last = SEQ - CHUNK
    qs = [qm_ref[0, h, last + BAND:SEQ, :] for h in range(2)]
    put(0, last + BAND, BAND, 1, _pair_attend(qs, k0_ref[last:SEQ, :], v0_ref[last:SEQ, :], bias_last))

    d1 = DILATIONS[1]
    nt = SEQ // TM
    for r in range(d1):
        base = [j * TM + r * CHUNK for j in range(nt)]
        nat = [j * TM + r for j in range(nt)]
        qs = [qm_ref[1, h, base[0]:base[0] + BAND, :] for h in range(2)]
        vals = _pair_attend(qs, k1_ref[base[0]:base[0] + CHUNK, :], v1_ref[base[0]:base[0] + CHUNK, :], bias_first)
        put(1, nat[0], BAND, d1, vals)
        for j in range(nt - 1):
            qs = [jnp.concatenate([qm_ref[1, h, base[j] + BAND:base[j] + CHUNK, :],
                                   qm_ref[1, h, base[j + 1]:base[j + 1] + BAND, :]], axis=0) for h in range(2)]
            k = jnp.concatenate([k1_ref[base[j]:base[j] + CHUNK, :], k1_ref[base[j + 1]:base[j + 1] + CHUNK, :]], axis=0)
            v = jnp.concatenate([v1_ref[base[j]:base[j] + CHUNK, :], v1_ref[base[j + 1]:base[j + 1] + CHUNK, :]], axis=0)
            acc, m, l = _pair_attend(qs, k, v, bias_full)
            put(1, nat[j] + BAND * d1, BAND, d1, (acc[:BAND], m[:BAND], l[:BAND]))
            put(1, nat[j + 1], BAND, d1, (acc[BAND:], m[BAND:], l[BAND:]))
        qs = [qm_ref[1, h, base[-1] + BAND:base[-1] + CHUNK, :] for h in range(2)]
        vals = _pair_attend(qs, k1_ref[base[-1]:base[-1] + CHUNK, :], v1_ref[base[-1]:base[-1] + CHUNK, :], bias_last)
        put(1, nat[-1] + BAND * d1, BAND, d1, vals)

    d2 = DILATIONS[2]
    piece = TM // d2

    def body2(r, carry):
        offs = [pl.multiple_of(j * TM + r * piece, piece) for j in range(nt)]
        qs = [jnp.concatenate([qm_ref[2, h, pl.ds(o, piece), :] for o in offs], axis=0) for h in range(2)]
        k = jnp.concatenate([k2_ref[pl.ds(o, piece), :] for o in offs], axis=0)
        v = jnp.concatenate([v2_ref[pl.ds(o, piece), :] for o in offs], axis=0)
        acc, m, l = _pair_attend(qs, k, v, bias_one)
        for j in range(nt):
            sl = slice(j * piece, (j + 1) * piece)
            put(2, j * TM + r, piece, d2, (acc[sl], m[sl], l[sl]))
        return carry

    lax.fori_loop(0, d2, body2, 0)

    rows = 256
    for c in range(SEQ // rows):
        sl = slice(c * rows, (c + 1) * rows)
        ms = [m_ref[g, sl, :] for g in range(3)]
        mx = jnp.maximum(jnp.maximum(ms[0], ms[1]), ms[2])
        num = None
        den = None
        for g in range(3):
            w = jnp.exp(ms[g] - mx)
            n_g = w * acc_ref[g, sl, :]
            d_g = w * l_ref[g, sl, :]
            num = n_g if num is None else num + n_g
            den = d_g if den is None else den + d_g
        o_ref[sl, :] = (num / den).astype(BF16)


def _dilated(a0, a1, a2):
    b = a0.shape[0]
    n_pairs = GROUP_W // LANES

    def col(off):
        return pl.BlockSpec((None, SEQ, LANES), lambda i, p: (i, 0, off + p))

    qkv = [col(0), col(n_pairs), col(2 * n_pairs)]
    return pl.pallas_call(
        _dilated_kernel,
        grid=(b, n_pairs),
        in_specs=qkv * 3,
        out_specs=pl.BlockSpec((None, SEQ, LANES), lambda i, p: (i, 0, p)),
        out_shape=jax.ShapeDtypeStruct((b, SEQ, GROUP_W), BF16),
        scratch_shapes=[pltpu.VMEM((3, 2, SEQ, LANES), BF16),
                        pltpu.VMEM((3, SEQ, LANES), F32),
                        pltpu.VMEM((3, SEQ, LANES), F32),
                        pltpu.VMEM((3, SEQ, LANES), F32)],
        compiler_params=pltpu.CompilerParams(dimension_semantics=("parallel", "parallel"),
                                             vmem_limit_bytes=VMEM_LIMIT_BYTES),
        name="dilated",
    )(a0, a0, a0, a1, a1, a1, a2, a2, a2)


def _na_key_row_start(a):
    return jnp.clip(NA_QROWS * a - NA_ROWS // 2, 0, GRID_ROWS - NA_KROWS)


def _na_case(a):
    n_blocks = GRID_ROWS // NA_QROWS
    return jnp.where(a < 2, a, jnp.where(a > n_blocks - 3, a - (n_blocks - NA_CASES), 2))


def _na_kernel(q_ref, k_ref, v_ref, bias_ref, o_ref, qm_ref):
    _split_heads(q_ref, qm_ref)
    qn = NA_QROWS * GRID_W

    def body(a, carry):
        k_off = pl.multiple_of(_na_key_row_start(a) * GRID_W, LANES)
        q_off = pl.multiple_of(a * qn, qn)
        case = _na_case(a)
        qs = [qm_ref[h, pl.ds(q_off, qn), :] for h in range(2)]
        bias = [bias_ref[h, case] for h in range(2)]
        acc, _, l = _pair_attend(qs, k_ref[pl.ds(k_off, NA_KEYS), :], v_ref[pl.ds(k_off, NA_KEYS), :], bias)
        o_ref[pl.ds(q_off, qn), :] = (acc / l).astype(BF16)
        return carry

    lax.fori_loop(0, GRID_ROWS // NA_QROWS, body, 0)


def _nbr(nb, bias_tab):
    b = nb.shape[0]
    n_pairs = NA_W // LANES

    def col(off):
        return pl.BlockSpec((None, SEQ, LANES), lambda p, i: (i, 0, off + p))

    return pl.pallas_call(
        _na_kernel,
        grid=(n_pairs, b),
        in_specs=[col(0), col(n_pairs), col(2 * n_pairs),
                  pl.BlockSpec((None, 2, NA_CASES, NA_QROWS * GRID_W, NA_KEYS), lambda p, i: (p, 0, 0, 0, 0))],
        out_specs=pl.BlockSpec((None, SEQ, LANES), lambda p, i: (i, 0, p)),
        out_shape=jax.ShapeDtypeStruct((b, SEQ, NA_W), BF16),
        scratch_shapes=[pltpu.VMEM((2, SEQ, LANES), BF16)],
        compiler_params=pltpu.CompilerParams(dimension_semantics=("parallel", "parallel"),
                                             vmem_limit_bytes=VMEM_LIMIT_BYTES),
        name="nbr",
    )(nb, nb, nb, bias_tab)


def _na_bias_table(rel_bias):
    n = GRID_W
    n_dr = 2 * NA_ROWS - 1
    pad_lo = (n - 1) - (NA_COLS - 1)
    v = jnp.pad(rel_bias.astype(F32), ((0, 0), (0, 0), (pad_lo, 2 * n - (2 * NA_COLS - 1) - pad_lo)))
    skew = jnp.broadcast_to(v[:, :, None, :], (NA_HEADS, n_dr, n, 2 * n)).reshape(NA_HEADS, n_dr, 2 * n * n)
    skew = skew[:, :, :n * (2 * n - 1)].reshape(NA_HEADS, n_dr, n, 2 * n - 1)
    toep = skew[:, :, :, n - 1:]
    qc = np.arange(n)
    win_lo = np.clip(qc - NA_COLS // 2, 0, n - NA_COLS)
    col_ok = (qc[None, :] >= win_lo[:, None]) & (qc[None, :] < win_lo[:, None] + NA_COLS)
    toep = jnp.where(col_ok, toep, NEG_INF)
    masked = jnp.full((NA_HEADS, n, n), NEG_INF, F32)

    n_blocks = GRID_ROWS // NA_QROWS
    cases = []
    for a in (0, 1, 2, n_blocks - 2, n_blocks - 1):
        start = int(np.clip(NA_QROWS * a - NA_ROWS // 2, 0, GRID_ROWS - NA_KROWS))
        q_rows = []
        for qi in range(NA_QROWS):
            qr = NA_QROWS * a + qi
            row_lo = int(np.clip(qr - NA_ROWS // 2, 0, GRID_ROWS - NA_ROWS))
            blocks = []
            for ki in range(NA_KROWS):
                kr = start + ki
                blocks.append(toep[:, kr - qr + NA_ROWS - 1] if row_lo <= kr < row_lo + NA_ROWS else masked)
            q_rows.append(jnp.concatenate(blocks, axis=-1))
        cases.append(jnp.concatenate(q_rows, axis=-2))
    tab = jnp.stack(cases, axis=1)
    return tab.reshape(NA_HEADS // 2, 2, NA_CASES, NA_QROWS * GRID_W, NA_KEYS)


def _out_ffn_kernel(x1_ref, ya_ref, yb_ref, gm_ref, wgate_ref, wa_ref, wb_ref, wout_ref,
                    g3_ref, wup_ref, wdown_ref, gf_ref, o_ref, *, final):
    x1 = x1_ref[...]
    h = _rms(x1, gm_ref[...]).astype(BF16)
    gates = jax.nn.sigmoid(jnp.dot(h, wgate_ref[...], preferred_element_type=F32))
    ba = jnp.dot(ya_ref[...], wa_ref[...], preferred_element_type=F32)
    bb = jnp.dot(yb_ref[...], wb_ref[...], preferred_element_type=F32)
    merged = gates[:, :D_MODEL] * ba + gates[:, D_MODEL:] * bb
    x2 = x1 + jnp.dot(merged.astype(BF16), wout_ref[...], preferred_element_type=F32)
    h3 = _rms(x2, g3_ref[...]).astype(BF16)
    x3 = x2 + 0.5 * _swiglu_half(h3, wup_ref, wdown_ref)
    if final:
        x3 = _rms(x3, gf_ref[...])
    o_ref[...] = x3


def _out_ffn(x1, ya, yb, gm, wgate, wa, wb, wout, g3, wup, wdown, gf, final):
    b = x1.shape[0]
    row = lambda w: pl.BlockSpec((None, TM, w), lambda i, j: (i, j, 0))
    weights = (gm, wgate, wa, wb, wout, g3, wup, wdown, gf)
    return pl.pallas_call(
        functools.partial(_out_ffn_kernel, final=final),
        grid=(b, SEQ // TM),
        in_specs=[row(D_MODEL), row(GROUP_W), row(NA_W)] + [_resident(w.shape) for w in weights],
        out_specs=row(D_MODEL),
        out_shape=jax.ShapeDtypeStruct((b, SEQ, D_MODEL), F32),
        compiler_params=pltpu.CompilerParams(dimension_semantics=("parallel", "parallel"),
                                             vmem_limit_bytes=VMEM_LIMIT_BYTES),
        name="out_ffn",
    )(x1, ya, yb, *weights)


def _rope_tables():
    half = HEAD_DIM // 2
    pos = jnp.arange(SEQ)
    inv_freq = ROPE_THETA ** (-jnp.arange(half, dtype=F32) / half)
    ang = pos.astype(F32)[:, None] * inv_freq[None, :]
    cos = jnp.cos(ang)
    sin = jnp.sin(ang)
    cos_slab = jnp.tile(cos, (1, LANES // half))
    sin_slab = jnp.tile(jnp.concatenate([-sin, sin], axis=-1), (1, LANES // HEAD_DIM))
    tables = [cos_slab, sin_slab]
    for d in DILATIONS[1:]:
        for t in (cos_slab, sin_slab):
            tables.append(t.reshape(SEQ // TM, TM // d, d, LANES).transpose(0, 2, 1, 3).reshape(SEQ, LANES))
    return tables


def kernel(x, ffn1_norm, ffn1_w_up, ffn1_w_down, mix_norm, w_in, na_rel_bias, w_branch_a, w_branch_b,
           w_out, ffn2_norm, ffn2_w_up, ffn2_w_down, final_norm):
    assert x.shape[1:] == (SEQ, D_MODEL) and x.dtype == F32
    tables = _rope_tables()
    gain = lambda g: g.astype(F32).reshape(1, D_MODEL)
    for layer in range(DEPTH):
        w = w_in[layer].astype(BF16)
        aq, ak, av = (w[:, i * 3 * GROUP_W:(i + 1) * 3 * GROUP_W] for i in range(3))
        grp = lambda t, g: t[:, g * GROUP_W:(g + 1) * GROUP_W]
        wnat = jnp.concatenate([grp(aq, 0), grp(ak, 0), grp(av, 0), w[:, DIL_QKV:DIL_QKV + NA_QKV]], axis=1)
        wg1 = jnp.concatenate([grp(aq, 1), grp(ak, 1), grp(av, 1)], axis=1)
        wg2 = jnp.concatenate([grp(aq, 2), grp(ak, 2), grp(av, 2)], axis=1)
        wgate = w[:, DIL_QKV + NA_QKV:]

        x1, a0, a1, a2, nb = _ffn_proj(
            x, gain(ffn1_norm[layer]), ffn1_w_up[layer].astype(BF16), ffn1_w_down[layer].astype(BF16),
            gain(mix_norm[layer]), wnat, wg1, wg2, tables)
        ya = _dilated(a0, a1, a2)
        yb = _nbr(nb, _na_bias_table(na_rel_bias[layer]))
        x = _out_ffn(x1, ya, yb, gain(mix_norm[layer]), wgate, w_branch_a[layer].astype(BF16),
                     w_branch_b[layer].astype(BF16), w_out[layer].astype(BF16), gain(ffn2_norm[layer]),
                     ffn2_w_up[layer].astype(BF16), ffn2_w_down[layer].astype(BF16), gain(final_norm),
                     final=(layer == DEPTH - 1))
    return x
```

```python
import functools

import numpy as np
import jax
import jax.numpy as jnp
from jax import lax
from jax.experimental import pallas as pl
from jax.experimental.pallas import tpu as pltpu

F32 = jnp.float32
BF16 = jnp.bfloat16

D_MODEL = 1024
SEQ = 2048
DEPTH = 2
HEAD_DIM = 64
DILATIONS = (1, 4, 16)
BAND = 64
GROUP_HEADS = 4
GROUP_W = GROUP_HEADS * HEAD_DIM
NA_HEADS = 8
NA_W = NA_HEADS * HEAD_DIM
GRID_W = 64
GRID_ROWS = SEQ // GRID_W
NA_ROWS = 8
NA_COLS = 16
D_FF = 2816
ROPE_THETA = 10000.0
RMS_EPS = 1e-6
NEG_INF = -1e30
DIL_QKV = 3 * 3 * GROUP_W
NA_QKV = 3 * NA_W

LANES = 128
VMEM_LIMIT_BYTES = 56 * 1024 * 1024

TM = 512
FF_CHUNK = 1408
NA_QROWS = 2
NA_KROWS = 10
NA_KEYS = NA_KROWS * GRID_W
DIL_PIPE_DEPTH = 4
NA_PIPE_DEPTH = 2
NA_CASES = 5


def _rms(x, g):
    return x * lax.rsqrt(jnp.mean(x * x, axis=-1, keepdims=True) + RMS_EPS) * g


def _swiglu_half(h, w_up_ref, w_down_ref):
    acc = None
    for c in range(D_FF // FF_CHUNK):
        lo = c * FF_CHUNK
        g = jnp.dot(h, w_up_ref[:, lo:lo + FF_CHUNK], preferred_element_type=F32)
        u = jnp.dot(h, w_up_ref[:, D_FF + lo:D_FF + lo + FF_CHUNK], preferred_element_type=F32)
        a = (g * jax.nn.sigmoid(g) * u).astype(BF16)
        part = jnp.dot(a, w_down_ref[lo:lo + FF_CHUNK, :], preferred_element_type=F32)
        acc = part if acc is None else acc + part
    return acc


def _rope(t, cos, sin_signed):
    lane = lax.broadcasted_iota(jnp.int32, t.shape, 1)
    first_half = (lane % HEAD_DIM) < (HEAD_DIM // 2)
    partner = jnp.where(first_half, pltpu.roll(t, LANES - HEAD_DIM // 2, 1), pltpu.roll(t, HEAD_DIM // 2, 1))
    return t * cos + partner * sin_signed


def _ffn_proj_kernel(x_ref, g1_ref, wup_ref, wdown_ref, g2_ref, wnat_ref, wknat_ref, wg1_ref, wkg1_ref, wg2_ref,
                     cos0_ref, sin0_ref, cos1_ref, sin1_ref, cos2_ref, sin2_ref,
                     cost0_ref, sint0_ref, cost1_ref, sint1_ref,
                     x1_ref, a0_ref, kt0_ref, a1_ref, kt1_ref, a2_ref, nb_ref, ktb_ref,
                     slab_ref, hperm_ref):
    x = x_ref[...]
    h = _rms(x, g1_ref[...]).astype(BF16)
    x1 = x + 0.5 * _swiglu_half(h, wup_ref, wdown_ref)
    x1_ref[...] = x1

    h2 = _rms(x1, g2_ref[...])
    n_slabs = D_MODEL // LANES
    for c in range(n_slabs):
        slab_ref[c] = h2[:, c * LANES:(c + 1) * LANES]

    scale = HEAD_DIM ** -0.5
    half = HEAD_DIM // 2
    nt_dims = (((1,), (1,)), ((), ()))

    def write_qv(a_ref, p, cos_ref, sin_ref):
        cos = cos_ref[...]
        sin = sin_ref[...]
        for c in range(GROUP_W // LANES):
            q = p[:, c * LANES:(c + 1) * LANES] * scale
            a_ref[:, c * LANES:(c + 1) * LANES] = _rope(q, cos, sin).astype(BF16)
        a_ref[:, GROUP_W:] = p[:, GROUP_W:].astype(BF16)

    def write_kt(kt_ref, kt, cost_ref, sint_ref):
        cos = cost_ref[...]
        sin = sint_ref[...]
        for hh in range(GROUP_HEADS):
            t1 = kt[hh * HEAD_DIM:hh * HEAD_DIM + half, :]
            t2 = kt[hh * HEAD_DIM + half:(hh + 1) * HEAD_DIM, :]
            kt_ref[hh * HEAD_DIM:hh * HEAD_DIM + half, :] = (t1 * cos - t2 * sin).astype(BF16)
            kt_ref[hh * HEAD_DIM + half:(hh + 1) * HEAD_DIM, :] = (t2 * cos + t1 * sin).astype(BF16)

    h2b = h2.astype(BF16)
    pn = jnp.dot(h2b, wnat_ref[...], preferred_element_type=F32)
    write_qv(a0_ref, pn[:, :2 * GROUP_W], cos0_ref, sin0_ref)
    nb_ref[:, :NA_W] = (pn[:, 2 * GROUP_W:2 * GROUP_W + NA_W] * scale).astype(BF16)
    nb_ref[:, NA_W:] = pn[:, 2 * GROUP_W + NA_W:].astype(BF16)
    ktn = lax.dot_general(wknat_ref[...], h2b, nt_dims, preferred_element_type=F32)
    write_kt(kt0_ref, ktn[:GROUP_W], cost0_ref, sint0_ref)
    ktb_ref[...] = ktn[GROUP_W:].astype(BF16)

    def permute(d):
        n = TM // d
        for r in range(d):
            for c in range(n_slabs):
                hperm_ref[r * n:(r + 1) * n, c * LANES:(c + 1) * LANES] = (
                    slab_ref[c, pl.ds(r, n, stride=d), :].astype(BF16))

    permute(DILATIONS[1])
    pg = jnp.dot(hperm_ref[...], wg1_ref[...], preferred_element_type=F32)
    write_qv(a1_ref, pg, cos1_ref, sin1_ref)
    kt1 = lax.dot_general(wkg1_ref[...], hperm_ref[...], nt_dims, preferred_element_type=F32)
    write_kt(kt1_ref, kt1, cost1_ref, sint1_ref)

    permute(DILATIONS[2])
    pg = jnp.dot(hperm_ref[...], wg2_ref[...], preferred_element_type=F32)
    cos = cos2_ref[...]
    sin = sin2_ref[...]
    for c in range(GROUP_W // LANES):
        q = pg[:, c * LANES:(c + 1) * LANES] * scale
        a2_ref[:, c * LANES:(c + 1) * LANES] = _rope(q, cos, sin).astype(BF16)
        k = pg[:, GROUP_W + c * LANES:GROUP_W + (c + 1) * LANES]
        a2_ref[:, GROUP_W + c * LANES:GROUP_W + (c + 1) * LANES] = _rope(k, cos, sin).astype(BF16)
    a2_ref[:, 2 * GROUP_W:] = pg[:, 2 * GROUP_W:].astype(BF16)


def _resident(shape):
    nd = len(shape)
    return pl.BlockSpec(shape, lambda *_: (0,) * nd, pipeline_mode=pl.Buffered(1))


def _ffn_proj(x, g1, wup, wdown, g2, wnat, wknat, wg1, wkg1, wg2, tables, tables_t):
    b = x.shape[0]
    nt = SEQ // TM
    row = lambda w: pl.BlockSpec((None, TM, w), lambda i, j: (i, j, 0))
    col = lambda h: pl.BlockSpec((None, h, TM), lambda i, j: (i, 0, j))
    tab = pl.BlockSpec((TM, LANES), lambda i, j: (j, 0))
    tab_t = pl.BlockSpec((HEAD_DIM // 2, TM), lambda i, j: (0, j))
    out_shape = (
        jax.ShapeDtypeStruct((b, SEQ, D_MODEL), F32),
        jax.ShapeDtypeStruct((b, SEQ, 2 * GROUP_W), BF16),
        jax.ShapeDtypeStruct((b, GROUP_W, SEQ), BF16),
        jax.ShapeDtypeStruct((b, SEQ, 2 * GROUP_W), BF16),
        jax.ShapeDtypeStruct((b, GROUP_W, SEQ), BF16),
        jax.ShapeDtypeStruct((b, SEQ, 3 * GROUP_W), BF16),
        jax.ShapeDtypeStruct((b, SEQ, 2 * NA_W), BF16),
        jax.ShapeDtypeStruct((b, NA_W, SEQ), BF16),
    )
    weights = (g1, wup, wdown, g2, wnat, wknat, wg1, wkg1, wg2)
    return pl.pallas_call(
        _ffn_proj_kernel,
        grid=(b, nt),
        in_specs=[row(D_MODEL)] + [_resident(w.shape) for w in weights] + [tab] * 6 + [tab_t] * 4,
        out_specs=(row(D_MODEL), row(2 * GROUP_W), col(GROUP_W), row(2 * GROUP_W), col(GROUP_W),
                   row(3 * GROUP_W), row(2 * NA_W), col(NA_W)),
        out_shape=out_shape,
        scratch_shapes=[pltpu.VMEM((D_MODEL // LANES, TM, LANES), F32), pltpu.VMEM((TM, D_MODEL), BF16)],
        compiler_params=pltpu.CompilerParams(dimension_semantics=("parallel", "parallel"),
                                             vmem_limit_bytes=VMEM_LIMIT_BYTES),
        name="ffn_proj",
    )(x, *weights, *tables, *tables_t)


def _split_heads(q_ref, qm_ref):
    q = q_ref[...].astype(F32)
    lane = lax.broadcasted_iota(jnp.int32, q.shape, 1)
    qm_ref[0] = jnp.where(lane < HEAD_DIM, q, 0.0).astype(BF16)
    qm_ref[1] = jnp.where(lane >= HEAD_DIM, q, 0.0).astype(BF16)


def _append_ones(v_ref, vx_ref):
    vx_ref[:, :LANES] = v_ref[...]
    vx_ref[:, LANES:] = jnp.ones(v_ref.shape, BF16)


def _pair_scores(q_pair, keys, bias, keys_feature_major=True):
    out = []
    for h in range(2):
        bh = bias[h] if isinstance(bias, (tuple, list)) else bias
        if keys_feature_major:
            s = jnp.dot(q_pair[h], keys, preferred_element_type=F32)
        else:
            s = lax.dot_general(q_pair[h], keys, (((1,), (1,)), ((), ())), preferred_element_type=F32)
        out.append(s + bh)
    return out


def _pair_finish(scores, vx):
    res = []
    for s in scores:
        m = jnp.max(s, axis=-1, keepdims=True)
        p = jnp.exp(s - m).astype(BF16)
        pv = jnp.dot(p, vx, preferred_element_type=F32)
        res.append((pv, m))
    shape = (res[0][0].shape[0], LANES)
    left = lax.broadcasted_iota(jnp.int32, shape, 1) < HEAD_DIM
    acc = jnp.where(left, res[0][0][:, :LANES], res[1][0][:, :LANES])
    l = jnp.where(left, res[0][0][:, LANES:], res[1][0][:, LANES:])
    m = jnp.where(left, jnp.broadcast_to(res[0][1], shape), jnp.broadcast_to(res[1][1], shape))
    return acc, m, l


def _software_pipeline(blocks, depth):
    pending = []
    for score_fn, finish_fn in blocks:
        pending.append((finish_fn, score_fn()))
        if len(pending) > depth:
            fn, sc = pending.pop(0)
            fn(sc)
    for fn, sc in pending:
        fn(sc)


CHUNK = 2 * BAND


def _band_bias(m, n, lo, hi):
    u = lax.broadcasted_iota(jnp.int32, (m, n), 0)
    w = lax.broadcasted_iota(jnp.int32, (m, n), 1)
    d = w - u
    return jnp.where(d >= lo, jnp.where(d <= hi, 0.0, NEG_INF), NEG_INF).astype(F32)


def _dilated_kernel(q0_ref, v0_ref, kt0_ref, q1_ref, v1_ref, kt1_ref, q2_ref, k2_ref, v2_ref,
                    o_ref, qm_ref, vx_ref, acc_ref, m_ref, l_ref):
    for g, (q_ref, v_ref) in enumerate(((q0_ref, v0_ref), (q1_ref, v1_ref), (q2_ref, v2_ref))):
        _split_heads(q_ref, qm_ref.at[g])
        _append_ones(v_ref, vx_ref.at[g])

    bias_first = _band_bias(BAND, CHUNK, -BAND, BAND)
    bias_last = _band_bias(BAND, CHUNK, 0, CHUNK)
    bias_full = _band_bias(CHUNK, 2 * CHUNK, 0, CHUNK)
    bias_one = _band_bias(CHUNK, CHUNK, -BAND, BAND)

    def put(g, dst, n, stride, vals):
        acc, m, l = vals
        if stride == 1:
            idx = pl.ds(dst, n)
        else:
            idx = pl.ds(dst, n, stride=stride)
        acc_ref[g, idx, :] = acc
        m_ref[g, idx, :] = m
        l_ref[g, idx, :] = l

    blocks = []

    def add(g, q_slices, keys_fn, vx_fn, bias, outs, feature_major=True):
        def score_fn():
            qs = [jnp.concatenate([qm_ref[g, h, sl, :] for sl in q_slices], axis=0) if len(q_slices) > 1
                  else qm_ref[g, h, q_slices[0], :] for h in range(2)]
            return _pair_scores(qs, keys_fn(), bias, feature_major)

        def finish_fn(scores):
            acc, m, l = _pair_finish(scores, vx_fn())
            for row0, n, dst, stride in outs:
                put(g, dst, n, stride, (acc[row0:row0 + n], m[row0:row0 + n], l[row0:row0 + n]))

        blocks.append((score_fn, finish_fn))

    def class_blocks(g, kt_ref, base, nat, stride):
        nc = len(base)
        rows = lambda j, lo, n: slice(base[j] + lo, base[j] + lo + n)
        add(g, [rows(0, 0, BAND)], lambda: kt_ref[:, rows(0, 0, CHUNK)], lambda: vx_ref[g, rows(0, 0, CHUNK), :],
            bias_first, [(0, BAND, nat[0], stride)])
        for j in range(nc - 1):
            add(g, [rows(j, BAND, BAND), rows(j + 1, 0, BAND)],
                lambda j=j: jnp.concatenate([kt_ref[:, rows(j, 0, CHUNK)], kt_ref[:, rows(j + 1, 0, CHUNK)]], axis=1),
                lambda j=j: jnp.concatenate([vx_ref[g, rows(j, 0, CHUNK), :], vx_ref[g, rows(j + 1, 0, CHUNK), :]], axis=0),
                bias_full, [(0, BAND, nat[j] + BAND * stride, stride), (BAND, BAND, nat[j + 1], stride)])
        add(g, [rows(nc - 1, BAND, BAND)], lambda: kt_ref[:, rows(nc - 1, 0, CHUNK)],
            lambda: vx_ref[g, rows(nc - 1, 0, CHUNK), :], bias_last, [(0, BAND, nat[nc - 1] + BAND * stride, stride)])

    chunk_rows = [c * CHUNK for c in range(SEQ // CHUNK)]
    class_blocks(0, kt0_ref, chunk_rows, chunk_rows, 1)

    d1 = DILATIONS[1]
    nt = SEQ // TM
    for r in range(d1):
        class_blocks(1, kt1_ref, [j * TM + r * CHUNK for j in range(nt)], [j * TM + r for j in range(nt)], d1)

    d2 = DILATIONS[2]
    piece = TM // d2
    for r in range(d2):
        pieces = [slice(j * TM + r * piece, j * TM + (r + 1) * piece) for j in range(nt)]
        add(2, pieces,
            lambda pieces=pieces: jnp.concatenate([k2_ref[sl, :] for sl in pieces], axis=0),
            lambda pieces=pieces: jnp.concatenate([vx_ref[2, sl, :] for sl in pieces], axis=0),
            bias_one, [(j * piece, piece, j * TM + r, d2) for j in range(nt)], feature_major=False)

    _software_pipeline(blocks, DIL_PIPE_DEPTH)

    rows = 256
    for c in range(SEQ // rows):
        sl = slice(c * rows, (c + 1) * rows)
        ms = [m_ref[g, sl, :] for g in range(3)]
        mx = jnp.maximum(jnp.maximum(ms[0], ms[1]), ms[2])
        num = None
        den = None
        for g in range(3):
            w = jnp.exp(ms[g] - mx)
            n_g = w * acc_ref[g, sl, :]
            d_g = w * l_ref[g, sl, :]
            num = n_g if num is None else num + n_g
            den = d_g if den is None else den + d_g
        o_ref[sl, :] = (num / den).astype(BF16)


def _dilated(a0, kt0, a1, kt1, a2):
    b = a0.shape[0]
    n_pairs = GROUP_W // LANES

    def col(off):
        return pl.BlockSpec((None, SEQ, LANES), lambda i, p: (i, 0, off + p))

    ktspec = pl.BlockSpec((None, LANES, SEQ), lambda i, p: (i, p, 0))
    return pl.pallas_call(
        _dilated_kernel,
        grid=(b, n_pairs),
        in_specs=[col(0), col(n_pairs), ktspec, col(0), col(n_pairs), ktspec,
                  col(0), col(n_pairs), col(2 * n_pairs)],
        out_specs=pl.BlockSpec((None, SEQ, LANES), lambda i, p: (i, 0, p)),
        out_shape=jax.ShapeDtypeStruct((b, SEQ, GROUP_W), BF16),
        scratch_shapes=[pltpu.VMEM((3, 2, SEQ, LANES), BF16),
                        pltpu.VMEM((3, SEQ, 2 * LANES), BF16),
                        pltpu.VMEM((3, SEQ, LANES), F32),
                        pltpu.VMEM((3, SEQ, LANES), F32),
                        pltpu.VMEM((3, SEQ, LANES), F32)],
        compiler_params=pltpu.CompilerParams(dimension_semantics=("parallel", "parallel"),
                                             vmem_limit_bytes=VMEM_LIMIT_BYTES),
        name="dilated",
    )(a0, a0, kt0, a1, a1, kt1, a2, a2, a2)


def _na_kernel(q_ref, v_ref, kt_ref, bias_ref, o_ref, qm_ref, vx_ref):
    _split_heads(q_ref, qm_ref)
    _append_ones(v_ref, vx_ref)
    qn = NA_QROWS * GRID_W

    n_blocks = GRID_ROWS // NA_QROWS
    blocks = []
    for a in range(n_blocks):
        k_off = int(np.clip(NA_QROWS * a - NA_ROWS // 2, 0, GRID_ROWS - NA_KROWS)) * GRID_W
        case = a if a < 2 else (a - (n_blocks - NA_CASES) if a > n_blocks - 3 else 2)
        qsl = slice(a * qn, (a + 1) * qn)
        ksl = slice(k_off, k_off + NA_KEYS)

        def score_fn(qsl=qsl, ksl=ksl, case=case):
            qs = [qm_ref[h, qsl, :] for h in range(2)]
            return _pair_scores(qs, kt_ref[:, ksl], [bias_ref[h, case] for h in range(2)])

        def finish_fn(scores, qsl=qsl, ksl=ksl):
            acc, _, l = _pair_finish(scores, vx_ref[ksl, :])
            o_ref[qsl, :] = (acc / l).astype(BF16)

        blocks.append((score_fn, finish_fn))
    _software_pipeline(blocks, NA_PIPE_DEPTH)


def _nbr(nb, ktb, bias_tab):
    b = nb.shape[0]
    n_pairs = NA_W // LANES

    def col(off):
        return pl.BlockSpec((None, SEQ, LANES), lambda p, i: (i, 0, off + p))

    return pl.pallas_call(
        _na_kernel,
        grid=(n_pairs, b),
        in_specs=[col(0), col(n_pairs), pl.BlockSpec((None, LANES, SEQ), lambda p, i: (i, p, 0)),
                  pl.BlockSpec((None, 2, NA_CASES, NA_QROWS * GRID_W, NA_KEYS), lambda p, i: (p, 0, 0, 0, 0))],
        out_specs=pl.BlockSpec((None, SEQ, LANES), lambda p, i: (i, 0, p)),
        out_shape=jax.ShapeDtypeStruct((b, SEQ, NA_W), BF16),
        scratch_shapes=[pltpu.VMEM((2, SEQ, LANES), BF16), pltpu.VMEM((SEQ, 2 * LANES), BF16)],
        compiler_params=pltpu.CompilerParams(dimension_semantics=("parallel", "parallel"),
                                             vmem_limit_bytes=VMEM_LIMIT_BYTES),
        name="nbr",
    )(nb, nb, ktb, bias_tab)


def _na_bias_table(rel_bias):
    n = GRID_W
    n_dr = 2 * NA_ROWS - 1
    pad_lo = (n - 1) - (NA_COLS - 1)
    v = jnp.pad(rel_bias.astype(F32), ((0, 0), (0, 0), (pad_lo, 2 * n - (2 * NA_COLS - 1) - pad_lo)))
    skew = jnp.broadcast_to(v[:, :, None, :], (NA_HEADS, n_dr, n, 2 * n)).reshape(NA_HEADS, n_dr, 2 * n * n)
    skew = skew[:, :, :n * (2 * n - 1)].reshape(NA_HEADS, n_dr, n, 2 * n - 1)
    toep = skew[:, :, :, n - 1:]
    qc = np.arange(n)
    win_lo = np.clip(qc - NA_COLS // 2, 0, n - NA_COLS)
    col_ok = (qc[None, :] >= win_lo[:, None]) & (qc[None, :] < win_lo[:, None] + NA_COLS)
    toep = jnp.where(col_ok, toep, NEG_INF)
    masked = jnp.full((NA_HEADS, n, n), NEG_INF, F32)

    n_blocks = GRID_ROWS // NA_QROWS
    cases = []
    for a in (0, 1, 2, n_blocks - 2, n_blocks - 1):
        start = int(np.clip(NA_QROWS * a - NA_ROWS // 2, 0, GRID_ROWS - NA_KROWS))
        q_rows = []
        for qi in range(NA_QROWS):
            qr = NA_QROWS * a + qi
            row_lo = int(np.clip(qr - NA_ROWS // 2, 0, GRID_ROWS - NA_ROWS))
            blocks = []
            for ki in range(NA_KROWS):
                kr = start + ki
                blocks.append(toep[:, kr - qr + NA_ROWS - 1] if row_lo <= kr < row_lo + NA_ROWS else masked)
            q_rows.append(jnp.concatenate(blocks, axis=-1))
        cases.append(jnp.concatenate(q_rows, axis=-2))
    tab = jnp.stack(cases, axis=1)
    return tab.reshape(NA_HEADS // 2, 2, NA_CASES, NA_QROWS * GRID_W, NA_KEYS)


def _out_ffn_kernel(x1_ref, ya_ref, yb_ref, gm_ref, wgate_ref, wa_ref, wb_ref, wout_ref,
                    g3_ref, wup_ref, wdown_ref, gf_ref, o_ref, *, final):
    x1 = x1_ref[...]
    h = _rms(x1, gm_ref[...]).astype(BF16)
    gates = jax.nn.sigmoid(jnp.dot(h, wgate_ref[...], preferred_element_type=F32))
    ba = jnp.dot(ya_ref[...], wa_ref[...], preferred_element_type=F32)
    bb = jnp.dot(yb_ref[...], wb_ref[...], preferred_element_type=F32)
    merged = gates[:, :D_MODEL] * ba + gates[:, D_MODEL:] * bb
    x2 = x1 + jnp.dot(merged.astype(BF16), wout_ref[...], preferred_element_type=F32)
    h3 = _rms(x2, g3_ref[...]).astype(BF16)
    x3 = x2 + 0.5 * _swiglu_half(h3, wup_ref, wdown_ref)
    if final:
        x3 = _rms(x3, gf_ref[...])
    o_ref[...] = x3


def _out_ffn(x1, ya, yb, gm, wgate, wa, wb, wout, g3, wup, wdown, gf, final):
    b = x1.shape[0]
    row = lambda w: pl.BlockSpec((None, TM, w), lambda i, j: (i, j, 0))
    weights = (gm, wgate, wa, wb, wout, g3, wup, wdown, gf)
    return pl.pallas_call(
        functools.partial(_out_ffn_kernel, final=final),
        grid=(b, SEQ // TM),
        in_specs=[row(D_MODEL), row(GROUP_W), row(NA_W)] + [_resident(w.shape) for w in weights],
        out_specs=row(D_MODEL),
        out_shape=jax.ShapeDtypeStruct((b, SEQ, D_MODEL), F32),
        compiler_params=pltpu.CompilerParams(dimension_semantics=("parallel", "parallel"),
                                             vmem_limit_bytes=VMEM_LIMIT_BYTES),
        name="out_ffn",
    )(x1, ya, yb, *weights)


def _rope_tables():
    half = HEAD_DIM // 2
    pos = jnp.arange(SEQ)
    inv_freq = ROPE_THETA ** (-jnp.arange(half, dtype=F32) / half)
    ang = pos.astype(F32)[:, None] * inv_freq[None, :]
    cos = jnp.cos(ang)
    sin = jnp.sin(ang)

    def class_major(t, d):
        w = t.shape[-1]
        return t.reshape(SEQ // TM, TM // d, d, w).transpose(0, 2, 1, 3).reshape(SEQ, w)

    tables, tables_t = [], []
    for d in DILATIONS:
        c, s = (cos, sin) if d == 1 else (class_major(cos, d), class_major(sin, d))
        tables.append(jnp.tile(c, (1, LANES // half)))
        tables.append(jnp.tile(jnp.concatenate([-s, s], axis=-1), (1, LANES // HEAD_DIM)))
        if d != DILATIONS[2]:
            tables_t += [c.T, s.T]
    return tables, tables_t


def kernel(x, ffn1_norm, ffn1_w_up, ffn1_w_down, mix_norm, w_in, na_rel_bias, w_branch_a, w_branch_b,
           w_out, ffn2_norm, ffn2_w_up, ffn2_w_down, final_norm):
    assert x.shape[1:] == (SEQ, D_MODEL) and x.dtype == F32
    tables, tables_t = _rope_tables()
    gain = lambda g: g.astype(F32).reshape(1, D_MODEL)
    for layer in range(DEPTH):
        w = w_in[layer].astype(BF16)
        aq, ak, av = (w[:, i * 3 * GROUP_W:(i + 1) * 3 * GROUP_W] for i in range(3))
        bq, bk, bv = (w[:, DIL_QKV + i * NA_W:DIL_QKV + (i + 1) * NA_W] for i in range(3))
        grp = lambda t, g: t[:, g * GROUP_W:(g + 1) * GROUP_W]
        wnat = jnp.concatenate([grp(aq, 0), grp(av, 0), bq, bv], axis=1)
        wknat = jnp.concatenate([grp(ak, 0), bk], axis=1).T
        wg1 = jnp.concatenate([grp(aq, 1), grp(av, 1)], axis=1)
        wkg1 = grp(ak, 1).T
        wg2 = jnp.concatenate([grp(aq, 2), grp(ak, 2), grp(av, 2)], axis=1)
        wgate = w[:, DIL_QKV + NA_QKV:]

        x1, a0, kt0, a1, kt1, a2, nb, ktb = _ffn_proj(
            x, gain(ffn1_norm[layer]), ffn1_w_up[layer].astype(BF16), ffn1_w_down[layer].astype(BF16),
            gain(mix_norm[layer]), wnat, wknat, wg1, wkg1, wg2, tables, tables_t)
        ya = _dilated(a0, kt0, a1, kt1, a2)
        yb = _nbr(nb, ktb, _na_bias_table(na_rel_bias[layer]))
        x = _out_ffn(x1, ya, yb, gain(mix_norm[layer]), wgate, w_branch_a[layer].astype(BF16),
                     w_branch_b[layer].astype(BF16), w_out[layer].astype(BF16), gain(ffn2_norm[layer]),
                     ffn2_w_up[layer].astype(BF16), ffn2_w_down[layer].astype(BF16), gain(final_norm),
                     final=(layer == DEPTH - 1))
    return x
```

```python
import functools

import numpy as np
import jax
import jax.numpy as jnp
from jax import lax
from jax.experimental import pallas as pl
from jax.experimental.pallas import tpu as pltpu

F32 = jnp.float32
BF16 = jnp.bfloat16

D_MODEL = 1024
SEQ = 2048
DEPTH = 2
HEAD_DIM = 64
DILATIONS = (1, 4, 16)
BAND = 64
GROUP_HEADS = 4
GROUP_W = GROUP_HEADS * HEAD_DIM
NA_HEADS = 8
NA_W = NA_HEADS * HEAD_DIM
GRID_W = 64
GRID_ROWS = SEQ // GRID_W
NA_ROWS = 8
NA_COLS = 16
D_FF = 2816
ROPE_THETA = 10000.0
RMS_EPS = 1e-6
NEG_INF = -1e30
DIL_QKV = 3 * 3 * GROUP_W
NA_QKV = 3 * NA_W

LANES = 128
VMEM_LIMIT_BYTES = 56 * 1024 * 1024

TM = 512
MXU_TILE = 256
FF_CHUNKS = (6 * MXU_TILE, 5 * MXU_TILE)
assert sum(FF_CHUNKS) == D_FF
NA_KEYS = NA_ROWS * GRID_W
DIL_PIPE_DEPTH = 4
NA_PIPE_DEPTH = 4
NA_CASE_ROWS = (0, 1, 2, 3, 4, 29, 30, 31)


def _rms(x, g):
    return x * lax.rsqrt(jnp.mean(x * x, axis=-1, keepdims=True) + RMS_EPS) * g


def _swiglu_half(h, w_up_ref, w_down_ref):
    acc = None
    lo = 0
    for width in FF_CHUNKS:
        g = jnp.dot(h, w_up_ref[:, lo:lo + width], preferred_element_type=F32)
        u = jnp.dot(h, w_up_ref[:, D_FF + lo:D_FF + lo + width], preferred_element_type=F32)
        a = (g * jax.nn.sigmoid(g) * u).astype(BF16)
        part = jnp.dot(a, w_down_ref[lo:lo + width, :], preferred_element_type=F32)
        acc = part if acc is None else acc + part
        lo += width
    return acc


def _rope(t, cos, sin_signed):
    lane = lax.broadcasted_iota(jnp.int32, t.shape, 1)
    first_half = (lane % HEAD_DIM) < (HEAD_DIM // 2)
    partner = jnp.where(first_half, pltpu.roll(t, LANES - HEAD_DIM // 2, 1), pltpu.roll(t, HEAD_DIM // 2, 1))
    return t * cos + partner * sin_signed


def _ffn_proj_kernel(x_ref, g1_ref, wup_ref, wdown_ref, g2_ref, wnat_ref, wknat_ref, wg1_ref, wkg1_ref, wg2_ref,
                     cos0_ref, sin0_ref, cos1_ref, sin1_ref, cos2_ref, sin2_ref,
                     cost0_ref, sint0_ref, cost1_ref, sint1_ref,
                     x1_ref, a0_ref, kt0_ref, a1_ref, kt1_ref, a2_ref, nb_ref, ktb_ref,
                     slab_ref, hperm_ref):
    x = x_ref[...]
    h = _rms(x, g1_ref[...]).astype(BF16)
    x1 = x + 0.5 * _swiglu_half(h, wup_ref, wdown_ref)
    x1_ref[...] = x1

    h2 = _rms(x1, g2_ref[...])
    n_slabs = D_MODEL // LANES
    for c in range(n_slabs):
        slab_ref[c] = h2[:, c * LANES:(c + 1) * LANES]

    scale = HEAD_DIM ** -0.5
    half = HEAD_DIM // 2
    nt_dims = (((1,), (1,)), ((), ()))

    def write_q(a_ref, q, width):
        lane = lax.broadcasted_iota(jnp.int32, q.shape, 1)
        first = (lane % LANES) < HEAD_DIM
        a_ref[:, :width] = jnp.where(first, q, 0.0).astype(BF16)
        a_ref[:, width:2 * width] = jnp.where(first, 0.0, q).astype(BF16)

    def rope_cols(p, lo, cos, sin, mul):
        return jnp.concatenate([_rope(p[:, lo + c * LANES:lo + (c + 1) * LANES] * mul, cos, sin)
                                for c in range(GROUP_W // LANES)], axis=1)

    def write_qv(a_ref, p, cos_ref, sin_ref):
        write_q(a_ref, rope_cols(p, 0, cos_ref[...], sin_ref[...], scale), GROUP_W)
        a_ref[:, 2 * GROUP_W:] = p[:, GROUP_W:].astype(BF16)

    def write_kt(kt_ref, kt, cost_ref, sint_ref):
        cos = cost_ref[...]
        sin = sint_ref[...]
        for hh in range(GROUP_HEADS):
            t1 = kt[hh * HEAD_DIM:hh * HEAD_DIM + half, :]
            t2 = kt[hh * HEAD_DIM + half:(hh + 1) * HEAD_DIM, :]
            kt_ref[hh * HEAD_DIM:hh * HEAD_DIM + half, :] = (t1 * cos - t2 * sin).astype(BF16)
            kt_ref[hh * HEAD_DIM + half:(hh + 1) * HEAD_DIM, :] = (t2 * cos + t1 * sin).astype(BF16)

    h2b = h2.astype(BF16)
    pn = jnp.dot(h2b, wnat_ref[...], preferred_element_type=F32)
    write_qv(a0_ref, pn[:, :2 * GROUP_W], cos0_ref, sin0_ref)
    write_q(nb_ref, pn[:, 2 * GROUP_W:2 * GROUP_W + NA_W] * scale, NA_W)
    nb_ref[:, 2 * NA_W:] = pn[:, 2 * GROUP_W + NA_W:].astype(BF16)
    ktn = lax.dot_general(wknat_ref[...], h2b, nt_dims, preferred_element_type=F32)
    write_kt(kt0_ref, ktn[:GROUP_W], cost0_ref, sint0_ref)
    ktb_ref[...] = ktn[GROUP_W:].astype(BF16)

    def permute(d):
        n = TM // d
        for r in range(d):
            for c in range(n_slabs):
                hperm_ref[r * n:(r + 1) * n, c * LANES:(c + 1) * LANES] = (
                    slab_ref[c, pl.ds(r, n, stride=d), :].astype(BF16))

    permute(DILATIONS[1])
    pg = jnp.dot(hperm_ref[...], wg1_ref[...], preferred_element_type=F32)
    write_qv(a1_ref, pg, cos1_ref, sin1_ref)
    kt1 = lax.dot_general(wkg1_ref[...], hperm_ref[...], nt_dims, preferred_element_type=F32)
    write_kt(kt1_ref, kt1, cost1_ref, sint1_ref)

    permute(DILATIONS[2])
    pg = jnp.dot(hperm_ref[...], wg2_ref[...], preferred_element_type=F32)
    write_q(a2_ref, rope_cols(pg, 0, cos2_ref[...], sin2_ref[...], scale), GROUP_W)
    a2_ref[:, 2 * GROUP_W:3 * GROUP_W] = rope_cols(pg, GROUP_W, cos2_ref[...], sin2_ref[...], 1.0).astype(BF16)
    a2_ref[:, 3 * GROUP_W:] = pg[:, 2 * GROUP_W:].astype(BF16)


def _resident(stacked_shape, layer):
    nd = len(stacked_shape) - 1
    return pl.BlockSpec((None,) + tuple(stacked_shape[1:]), lambda *_: (layer,) + (0,) * nd,
                        pipeline_mode=pl.Buffered(1))


def _ffn_proj(layer, x, g1, wup, wdown, g2, wnat, wknat, wg1, wkg1, wg2, tables, tables_t):
    b = x.shape[0]
    nt = SEQ // TM
    row = lambda w: pl.BlockSpec((None, TM, w), lambda i, j: (i, j, 0))
    col = lambda h: pl.BlockSpec((None, h, TM), lambda i, j: (i, 0, j))
    tab = pl.BlockSpec((TM, LANES), lambda i, j: (j, 0))
    tab_t = pl.BlockSpec((HEAD_DIM // 2, TM), lambda i, j: (0, j))
    out_shape = (
        jax.ShapeDtypeStruct((b, SEQ, D_MODEL), F32),
        jax.ShapeDtypeStruct((b, SEQ, 3 * GROUP_W), BF16),
        jax.ShapeDtypeStruct((b, GROUP_W, SEQ), BF16),
        jax.ShapeDtypeStruct((b, SEQ, 3 * GROUP_W), BF16),
        jax.ShapeDtypeStruct((b, GROUP_W, SEQ), BF16),
        jax.ShapeDtypeStruct((b, SEQ, 4 * GROUP_W), BF16),
        jax.ShapeDtypeStruct((b, SEQ, 3 * NA_W), BF16),
        jax.ShapeDtypeStruct((b, NA_W, SEQ), BF16),
    )
    weights = (g1, wup, wdown, g2, wnat, wknat, wg1, wkg1, wg2)
    return pl.pallas_call(
        _ffn_proj_kernel,
        grid=(b, nt),
        in_specs=[row(D_MODEL)] + [_resident(w.shape, layer) for w in weights] + [tab] * 6 + [tab_t] * 4,
        out_specs=(row(D_MODEL), row(3 * GROUP_W), col(GROUP_W), row(3 * GROUP_W), col(GROUP_W),
                   row(4 * GROUP_W), row(3 * NA_W), col(NA_W)),
        out_shape=out_shape,
        scratch_shapes=[pltpu.VMEM((D_MODEL // LANES, TM, LANES), F32), pltpu.VMEM((TM, D_MODEL), BF16)],
        compiler_params=pltpu.CompilerParams(dimension_semantics=("parallel", "parallel"),
                                             vmem_limit_bytes=VMEM_LIMIT_BYTES),
        name="ffn_proj",
    )(x, *weights, *tables, *tables_t)


def _pair_scores(q_stack, keys, bias, keys_feature_major=True, stacked=False):
    half = q_stack.shape[0] // 2
    parts = [(q_stack, bias)] if stacked else [(q_stack[h * half:(h + 1) * half], bias[h * half:(h + 1) * half])
                                               for h in range(2)]
    out = []
    for q, b in parts:
        if keys_feature_major:
            s = jnp.dot(q, keys, preferred_element_type=F32)
        else:
            s = lax.dot_general(q, keys, (((1,), (1,)), ((), ())), preferred_element_type=F32)
        out.append(s + b)
    return out


def _pair_finish(scores, v):
    vx = jnp.concatenate([v, jnp.ones(v.shape, BF16)], axis=1)
    res = []
    for sc in scores:
        m = jnp.max(sc, axis=-1, keepdims=True)
        p = jnp.exp(sc - m).astype(BF16)
        res.append((jnp.dot(p, vx, preferred_element_type=F32), m))
    if len(res) == 1:
        half = res[0][0].shape[0] // 2
        res = [(res[0][0][h * half:(h + 1) * half], res[0][1][h * half:(h + 1) * half]) for h in range(2)]
    shape = (res[0][0].shape[0], LANES)
    left = lax.broadcasted_iota(jnp.int32, shape, 1) < HEAD_DIM
    acc = jnp.where(left, res[0][0][:, :LANES], res[1][0][:, :LANES])
    l = jnp.where(left, res[0][0][:, LANES:], res[1][0][:, LANES:])
    m = jnp.where(left, jnp.broadcast_to(res[0][1], shape), jnp.broadcast_to(res[1][1], shape))
    return acc, m, l


def _software_pipeline(blocks, depth):
    pending = []
    for score_fn, finish_fn in blocks:
        pending.append((finish_fn, score_fn()))
        if len(pending) > depth:
            fn, sc = pending.pop(0)
            fn(sc)
    for fn, sc in pending:
        fn(sc)


CHUNK = 2 * BAND


def _band_bias(m, n, lo, hi):
    u = lax.broadcasted_iota(jnp.int32, (m, n), 0)
    w = lax.broadcasted_iota(jnp.int32, (m, n), 1)
    d = w - u
    return jnp.where(d >= lo, jnp.where(d <= hi, 0.0, NEG_INF), NEG_INF).astype(F32)


def _dilated_kernel(qa0_ref, qb0_ref, v0_ref, kt0_ref, qa1_ref, qb1_ref, v1_ref, kt1_ref,
                    qa2_ref, qb2_ref, k2_ref, v2_ref, o_ref, out_s, lse_s):
    q_refs = ((qa0_ref, qb0_ref), (qa1_ref, qb1_ref), (qa2_ref, qb2_ref))
    v_refs = (v0_ref, v1_ref, v2_ref)

    both = lambda b: jnp.concatenate([b, b], axis=0)
    bias_first = both(_band_bias(BAND, CHUNK, -BAND, BAND))
    bias_last = both(_band_bias(BAND, CHUNK, 0, CHUNK))
    bias_full = both(_band_bias(CHUNK, 2 * CHUNK, 0, CHUNK))
    bias_one = both(_band_bias(CHUNK, CHUNK, -BAND, BAND))

    def normalise(vals):
        acc, m, l = vals
        return acc / l, m + jnp.log(l)

    def stash(g, outs, vals):
        o, lse = normalise(vals)
        for row0, n, dst, stride in outs:
            idx = pl.ds(dst, n, stride=stride)
            out_s[g - 1, idx, :] = o[row0:row0 + n]
            lse_s[g - 1, idx, :] = lse[row0:row0 + n]

    def merge(g, outs, vals):
        assert g == 0
        o0, lse0 = normalise(vals)
        for row0, n, dst, stride in outs:
            assert stride == 1
            sl = slice(dst, dst + n)
            os_ = [o0[row0:row0 + n], out_s[0, sl, :], out_s[1, sl, :]]
            ls_ = [lse0[row0:row0 + n], lse_s[0, sl, :], lse_s[1, sl, :]]
            mx = jnp.maximum(jnp.maximum(ls_[0], ls_[1]), ls_[2])
            ws = [jnp.exp(x - mx) for x in ls_]
            num = ws[0] * os_[0] + ws[1] * os_[1] + ws[2] * os_[2]
            o_ref[sl, :] = (num / (ws[0] + ws[1] + ws[2])).astype(BF16)

    blocks = []

    def add(g, q_slices, keys_fn, v_slices, bias, outs, feature_major=True):
        def score_fn():
            q_stack = jnp.concatenate([q_ref[sl, :] for q_ref in q_refs[g] for sl in q_slices], axis=0)
            return _pair_scores(q_stack, keys_fn(), bias, feature_major)

        def finish_fn(scores):
            v = jnp.concatenate([v_refs[g][sl, :] for sl in v_slices], axis=0)
            (merge if g == 0 else stash)(g, outs, _pair_finish(scores, v))

        blocks.append((score_fn, finish_fn))

    def class_blocks(g, kt_ref, base, nat, stride):
        nc = len(base)
        rows = lambda j, lo, n: slice(base[j] + lo, base[j] + lo + n)
        add(g, [rows(0, 0, BAND)], lambda: kt_ref[:, rows(0, 0, CHUNK)], [rows(0, 0, CHUNK)],
            bias_first, [(0, BAND, nat[0], stride)])
        for j in range(nc - 1):
            add(g, [rows(j, BAND, BAND), rows(j + 1, 0, BAND)],
                lambda j=j: jnp.concatenate([kt_ref[:, rows(j, 0, CHUNK)], kt_ref[:, rows(j + 1, 0, CHUNK)]], axis=1),
                [rows(j, 0, CHUNK), rows(j + 1, 0, CHUNK)],
                bias_full, [(0, BAND, nat[j] + BAND * stride, stride), (BAND, BAND, nat[j + 1], stride)])
        add(g, [rows(nc - 1, BAND, BAND)], lambda: kt_ref[:, rows(nc - 1, 0, CHUNK)],
            [rows(nc - 1, 0, CHUNK)], bias_last, [(0, BAND, nat[nc - 1] + BAND * stride, stride)])

    nt = SEQ // TM
    d2 = DILATIONS[2]
    piece = TM // d2
    for r in range(d2):
        pieces = [slice(j * TM + r * piece, j * TM + (r + 1) * piece) for j in range(nt)]
        add(2, pieces, lambda pieces=pieces: jnp.concatenate([k2_ref[sl, :] for sl in pieces], axis=0), pieces,
            bias_one, [(j * piece, piece, j * TM + r, d2) for j in range(nt)], feature_major=False)

    d1 = DILATIONS[1]
    for r in range(d1):
        class_blocks(1, kt1_ref, [j * TM + r * CHUNK for j in range(nt)], [j * TM + r for j in range(nt)], d1)

    chunk_rows = [c * CHUNK for c in range(SEQ // CHUNK)]
    class_blocks(0, kt0_ref, chunk_rows, chunk_rows, 1)

    _software_pipeline(blocks, DIL_PIPE_DEPTH)


def _dilated(a0, kt0, a1, kt1, a2):
    b = a0.shape[0]
    n_pairs = GROUP_W // LANES

    def col(off):
        return pl.BlockSpec((None, SEQ, LANES), lambda i, p: (i, 0, off + p))

    ktspec = pl.BlockSpec((None, LANES, SEQ), lambda i, p: (i, p, 0))
    return pl.pallas_call(
        _dilated_kernel,
        grid=(b, n_pairs),
        in_specs=[col(0), col(n_pairs), col(2 * n_pairs), ktspec,
                  col(0), col(n_pairs), col(2 * n_pairs), ktspec,
                  col(0), col(n_pairs), col(2 * n_pairs), col(3 * n_pairs)],
        out_specs=pl.BlockSpec((None, SEQ, LANES), lambda i, p: (i, 0, p)),
        out_shape=jax.ShapeDtypeStruct((b, SEQ, GROUP_W), BF16),
        scratch_shapes=[pltpu.VMEM((2, SEQ, LANES), F32), pltpu.VMEM((2, SEQ, LANES), F32)],
        compiler_params=pltpu.CompilerParams(dimension_semantics=("parallel", "parallel"),
                                             vmem_limit_bytes=VMEM_LIMIT_BYTES),
        name="dilated",
    )(a0, a0, a0, kt0, a1, a1, a1, kt1, a2, a2, a2, a2)


def _na_window_start(i):
    return int(np.clip(i - NA_ROWS // 2, 0, GRID_ROWS - NA_ROWS))


def _na_kernel(qa_ref, qb_ref, v_ref, kt_ref, bias_ref, o_ref, kts_ref):
    words = pltpu.bitcast(kt_ref[...], jnp.uint32)
    kts_ref[...] = pltpu.bitcast(pltpu.roll(words, SEQ - GRID_W, 1), BF16)

    blocks = []
    for i in range(GRID_ROWS):
        lo = _na_window_start(i)
        case = NA_CASE_ROWS.index(i) if i in NA_CASE_ROWS else NA_CASE_ROWS.index(NA_ROWS // 2)
        qsl = slice(i * GRID_W, (i + 1) * GRID_W)
        vsl = slice(lo * GRID_W, lo * GRID_W + NA_KEYS)
        if lo % 2 == 0:
            keys_fn = lambda lo=lo: kt_ref[:, lo * GRID_W:lo * GRID_W + NA_KEYS]
        else:
            keys_fn = lambda lo=lo: kts_ref[:, (lo - 1) * GRID_W:(lo - 1) * GRID_W + NA_KEYS]

        def score_fn(qsl=qsl, keys_fn=keys_fn, case=case):
            q_stack = jnp.concatenate([qa_ref[qsl, :], qb_ref[qsl, :]], axis=0)
            return _pair_scores(q_stack, keys_fn(), bias_ref[case], stacked=True)

        def finish_fn(scores, qsl=qsl, vsl=vsl):
            acc, _, l = _pair_finish(scores, v_ref[vsl, :])
            o_ref[qsl, :] = (acc / l).astype(BF16)

        blocks.append((score_fn, finish_fn))
    _software_pipeline(blocks, NA_PIPE_DEPTH)


def _nbr(layer, nb, ktb, bias_tab):
    b = nb.shape[0]
    n_pairs = NA_W // LANES

    def col(off):
        return pl.BlockSpec((None, SEQ, LANES), lambda p, i: (i, 0, off + p))

    return pl.pallas_call(
        _na_kernel,
        grid=(n_pairs, b),
        in_specs=[col(0), col(n_pairs), col(2 * n_pairs), pl.BlockSpec((None, LANES, SEQ), lambda p, i: (i, p, 0)),
                  pl.BlockSpec((None, None, len(NA_CASE_ROWS), 2 * GRID_W, NA_KEYS),
                               lambda p, i: (layer, p, 0, 0, 0))],
        out_specs=pl.BlockSpec((None, SEQ, LANES), lambda p, i: (i, 0, p)),
        out_shape=jax.ShapeDtypeStruct((b, SEQ, NA_W), BF16),
        scratch_shapes=[pltpu.VMEM((LANES, SEQ), BF16)],
        compiler_params=pltpu.CompilerParams(dimension_semantics=("parallel", "parallel"),
                                             vmem_limit_bytes=VMEM_LIMIT_BYTES),
        name="nbr",
    )(nb, nb, nb, ktb, bias_tab)


def _na_bias_table(rel_bias):
    n = GRID_W
    n_dr = 2 * NA_ROWS - 1
    pad_lo = (n - 1) - (NA_COLS - 1)
    nl = rel_bias.shape[0]
    flat = rel_bias.astype(F32).reshape(nl * NA_HEADS, n_dr, 2 * NA_COLS - 1)
    v = jnp.pad(flat, ((0, 0), (0, 0), (pad_lo, 2 * n - (2 * NA_COLS - 1) - pad_lo)))
    nh = nl * NA_HEADS
    skew = jnp.broadcast_to(v[:, :, None, :], (nh, n_dr, n, 2 * n)).reshape(nh, n_dr, 2 * n * n)
    skew = skew[:, :, :n * (2 * n - 1)].reshape(nh, n_dr, n, 2 * n - 1)
    toep = skew[:, :, :, n - 1:]
    qc = np.arange(n)
    win_lo = np.clip(qc - NA_COLS // 2, 0, n - NA_COLS)
    col_ok = (qc[None, :] >= win_lo[:, None]) & (qc[None, :] < win_lo[:, None] + NA_COLS)
    toep = jnp.where(col_ok, toep, NEG_INF)

    cases = []
    for i in NA_CASE_ROWS:
        lo = _na_window_start(i)
        cases.append(jnp.concatenate([toep[:, lo + ki - i + NA_ROWS - 1] for ki in range(NA_ROWS)], axis=-1))
    tab = jnp.stack(cases, axis=1)
    nc = len(NA_CASE_ROWS)
    tab = tab.reshape(nl, NA_HEADS // 2, 2, nc, n, NA_KEYS).transpose(0, 1, 3, 2, 4, 5)
    return tab.reshape(nl, NA_HEADS // 2, nc, 2 * n, NA_KEYS)


def _out_ffn_kernel(x1_ref, ya_ref, yb_ref, gm_ref, wgate_ref, wa_ref, wb_ref, wout_ref,
                    g3_ref, wup_ref, wdown_ref, gf_ref, o_ref, *, final):
    x1 = x1_ref[...]
    h = _rms(x1, gm_ref[...]).astype(BF16)
    gates = jax.nn.sigmoid(jnp.dot(h, wgate_ref[...], preferred_element_type=F32))
    ba = jnp.dot(ya_ref[...], wa_ref[...], preferred_element_type=F32)
    bb = jnp.dot(yb_ref[...], wb_ref[...], preferred_element_type=F32)
    merged = gates[:, :D_MODEL] * ba + gates[:, D_MODEL:] * bb
    x2 = x1 + jnp.dot(merged.astype(BF16), wout_ref[...], preferred_element_type=F32)
    h3 = _rms(x2, g3_ref[...]).astype(BF16)
    x3 = x2 + 0.5 * _swiglu_half(h3, wup_ref, wdown_ref)
    if final:
        x3 = _rms(x3, gf_ref[...])
    o_ref[...] = x3


def _out_ffn(layer, x1, ya, yb, gm, wgate, wa, wb, wout, g3, wup, wdown, gf, final):
    b = x1.shape[0]
    row = lambda w: pl.BlockSpec((None, TM, w), lambda i, j: (i, j, 0))
    weights = (gm, wgate, wa, wb, wout, g3, wup, wdown)
    gf_spec = pl.BlockSpec(gf.shape, lambda i, j: (0, 0))
    return pl.pallas_call(
        functools.partial(_out_ffn_kernel, final=final),
        grid=(b, SEQ // TM),
        in_specs=[row(D_MODEL), row(GROUP_W), row(NA_W)] + [_resident(w.shape, layer) for w in weights] + [gf_spec],
        out_specs=row(D_MODEL),
        out_shape=jax.ShapeDtypeStruct((b, SEQ, D_MODEL), F32),
        compiler_params=pltpu.CompilerParams(dimension_semantics=("parallel", "parallel"),
                                             vmem_limit_bytes=VMEM_LIMIT_BYTES),
        name="out_ffn",
    )(x1, ya, yb, *weights, gf)


def _rope_tables():
    half = HEAD_DIM // 2
    pos = jnp.arange(SEQ)
    inv_freq = ROPE_THETA ** (-jnp.arange(half, dtype=F32) / half)
    ang = pos.astype(F32)[:, None] * inv_freq[None, :]
    cos = jnp.cos(ang)
    sin = jnp.sin(ang)

    def class_major(t, d):
        w = t.shape[-1]
        return t.reshape(SEQ // TM, TM // d, d, w).transpose(0, 2, 1, 3).reshape(SEQ, w)

    tables, tables_t = [], []
    for d in DILATIONS:
        c, s = (cos, sin) if d == 1 else (class_major(cos, d), class_major(sin, d))
        tables.append(jnp.tile(c, (1, LANES // half)))
        tables.append(jnp.tile(jnp.concatenate([-s, s], axis=-1), (1, LANES // HEAD_DIM)))
        if d != DILATIONS[2]:
            tables_t += [c.T, s.T]
    return tables, tables_t


def kernel(x, ffn1_norm, ffn1_w_up, ffn1_w_down, mix_norm, w_in, na_rel_bias, w_branch_a, w_branch_b,
           w_out, ffn2_norm, ffn2_w_up, ffn2_w_down, final_norm):
    assert x.shape[1:] == (SEQ, D_MODEL) and x.dtype == F32
    tables, tables_t = _rope_tables()
    gain = lambda g: g.astype(F32).reshape(-1, 1, D_MODEL)
    w = w_in.astype(BF16)
    aq, ak, av = (w[:, :, i * 3 * GROUP_W:(i + 1) * 3 * GROUP_W] for i in range(3))
    bq, bk, bv = (w[:, :, DIL_QKV + i * NA_W:DIL_QKV + (i + 1) * NA_W] for i in range(3))
    grp = lambda t, g: t[:, :, g * GROUP_W:(g + 1) * GROUP_W]
    wnat = jnp.concatenate([grp(aq, 0), grp(av, 0), bq, bv], axis=2)
    wknat = jnp.concatenate([grp(ak, 0), bk], axis=2).transpose(0, 2, 1)
    wg1 = jnp.concatenate([grp(aq, 1), grp(av, 1)], axis=2)
    wkg1 = grp(ak, 1).transpose(0, 2, 1)
    wg2 = jnp.concatenate([grp(aq, 2), grp(ak, 2), grp(av, 2)], axis=2)
    wgate = w[:, :, DIL_QKV + NA_QKV:]
    g_ffn1, g_mix, g_ffn2 = gain(ffn1_norm), gain(mix_norm), gain(ffn2_norm)
    up1, down1 = ffn1_w_up.astype(BF16), ffn1_w_down.astype(BF16)
    up2, down2 = ffn2_w_up.astype(BF16), ffn2_w_down.astype(BF16)
    wa, wb, wo = w_branch_a.astype(BF16), w_branch_b.astype(BF16), w_out.astype(BF16)
    bias_tab = _na_bias_table(na_rel_bias)
    g_final = final_norm.astype(F32).reshape(1, D_MODEL)

    for layer in range(DEPTH):
        x1, a0, kt0, a1, kt1, a2, nb, ktb = _ffn_proj(
            layer, x, g_ffn1, up1, down1, g_mix, wnat, wknat, wg1, wkg1, wg2, tables, tables_t)
        ya = _dilated(a0, kt0, a1, kt1, a2)
        yb = _nbr(layer, nb, ktb, bias_tab)
        x = _out_ffn(layer, x1, ya, yb, g_mix, wgate, wa, wb, wo, g_ffn2, up2, down2, g_final,
                     final=(layer == DEPTH - 1))
    return x
```

```python
import functools

import numpy as np
import jax
import jax.numpy as jnp
from jax import lax
from jax.experimental import pallas as pl
from jax.experimental.pallas import tpu as pltpu

F32 = jnp.float32
BF16 = jnp.bfloat16

D_MODEL = 1024
SEQ = 2048
DEPTH = 2
HEAD_DIM = 64
DILATIONS = (1, 4, 16)
BAND = 64
GROUP_HEADS = 4
GROUP_W = GROUP_HEADS * HEAD_DIM
NA_HEADS = 8
NA_W = NA_HEADS * HEAD_DIM
GRID_W = 64
GRID_ROWS = SEQ // GRID_W
NA_ROWS = 8
NA_COLS = 16
D_FF = 2816
ROPE_THETA = 10000.0
RMS_EPS = 1e-6
NEG_INF = -1e30
DIL_QKV = 3 * 3 * GROUP_W
NA_QKV = 3 * NA_W

LANES = 128
VMEM_LIMIT_BYTES = 56 * 1024 * 1024

TM = 512
MXU_TILE = 256
FF_CHUNKS = (6 * MXU_TILE, 5 * MXU_TILE)
assert sum(FF_CHUNKS) == D_FF
NA_KEYS = NA_ROWS * GRID_W
DIL_PIPE_DEPTH = 4
NA_PIPE_DEPTH = 4
NA_CASE_ROWS = (0, 1, 2, 3, 4, 29, 30, 31)


def _rms(x, g):
    return x * lax.rsqrt(jnp.mean(x * x, axis=-1, keepdims=True) + RMS_EPS) * g


def _swiglu_half(h, w_up_ref, w_down_ref):
    acc = None
    lo = 0
    for width in FF_CHUNKS:
        g = jnp.dot(h, w_up_ref[:, lo:lo + width], preferred_element_type=F32)
        u = jnp.dot(h, w_up_ref[:, D_FF + lo:D_FF + lo + width], preferred_element_type=F32)
        a = (g * jax.nn.sigmoid(g) * u).astype(BF16)
        part = jnp.dot(a, w_down_ref[lo:lo + width, :], preferred_element_type=F32)
        acc = part if acc is None else acc + part
        lo += width
    return acc


def _rope(t, cos, sin_signed):
    lane = lax.broadcasted_iota(jnp.int32, t.shape, 1)
    first_half = (lane % HEAD_DIM) < (HEAD_DIM // 2)
    partner = jnp.where(first_half, pltpu.roll(t, LANES - HEAD_DIM // 2, 1), pltpu.roll(t, HEAD_DIM // 2, 1))
    return t * cos + partner * sin_signed


def _ffn_proj_kernel(x_ref, g1_ref, wup_ref, wdown_ref, g2_ref, wnat_ref, wknat_ref, wg1_ref, wkg1_ref, wg2_ref,
                     tab_ref, tabt_ref,
                     x1_ref, a0_ref, kt0_ref, a1_ref, kt1_ref, a2_ref, nb_ref, ktb_ref,
                     slab_ref, hperm_ref):
    x = x_ref[...]
    h = _rms(x, g1_ref[...]).astype(BF16)
    x1 = x + 0.5 * _swiglu_half(h, wup_ref, wdown_ref)
    x1_ref[...] = x1

    h2 = _rms(x1, g2_ref[...])
    n_slabs = D_MODEL // LANES
    for c in range(n_slabs):
        slab_ref[c] = h2[:, c * LANES:(c + 1) * LANES]

    scale = HEAD_DIM ** -0.5
    half = HEAD_DIM // 2
    nt_dims = (((1,), (1,)), ((), ()))

    def tables(order):
        cos = tab_ref[:, (2 * order) * LANES:(2 * order + 1) * LANES]
        sin = tab_ref[:, (2 * order + 1) * LANES:(2 * order + 2) * LANES]
        return cos, sin

    def put_slabs(a_ref, first, t):
        for c in range(t.shape[1] // LANES):
            a_ref[first + c] = t[:, c * LANES:(c + 1) * LANES].astype(BF16)

    def head_masked(t):
        lane = lax.broadcasted_iota(jnp.int32, t.shape, 1)
        first = (lane % LANES) < HEAD_DIM
        return jnp.where(first, t, 0.0), jnp.where(first, 0.0, t)

    def rope_cols(p, lo, order, mul):
        cos, sin = tables(order)
        return jnp.concatenate([_rope(p[:, lo + c * LANES:lo + (c + 1) * LANES] * mul, cos, sin)
                                for c in range(GROUP_W // LANES)], axis=1)

    def write_kt(kt_ref, kt, order):
        cos = tabt_ref[(2 * order) * half:(2 * order + 1) * half, :]
        sin = tabt_ref[(2 * order + 1) * half:(2 * order + 2) * half, :]
        for hh in range(GROUP_HEADS):
            t1 = kt[hh * HEAD_DIM:hh * HEAD_DIM + half, :]
            t2 = kt[hh * HEAD_DIM + half:(hh + 1) * HEAD_DIM, :]
            kt_ref[hh * HEAD_DIM:hh * HEAD_DIM + half, :] = (t1 * cos - t2 * sin).astype(BF16)
            kt_ref[hh * HEAD_DIM + half:(hh + 1) * HEAD_DIM, :] = (t2 * cos + t1 * sin).astype(BF16)

    n_pairs = GROUP_W // LANES
    nb_pairs = NA_W // LANES

    h2b = h2.astype(BF16)
    pn = jnp.dot(h2b, wnat_ref[...], preferred_element_type=F32)
    put_slabs(a0_ref, 0, rope_cols(pn, 0, 0, scale))
    put_slabs(a0_ref, n_pairs, pn[:, GROUP_W:2 * GROUP_W])
    qb_a, qb_b = head_masked(pn[:, 2 * GROUP_W:2 * GROUP_W + NA_W] * scale)
    put_slabs(nb_ref, 0, qb_a)
    put_slabs(nb_ref, nb_pairs, qb_b)
    put_slabs(nb_ref, 2 * nb_pairs, pn[:, 2 * GROUP_W + NA_W:])
    ktn = lax.dot_general(wknat_ref[...], h2b, nt_dims, preferred_element_type=F32)
    write_kt(kt0_ref, ktn[:GROUP_W], 0)
    ktb_ref[...] = ktn[GROUP_W:].astype(BF16)

    def permute(d):
        n = TM // d
        for r in range(d):
            for c in range(n_slabs):
                hperm_ref[r * n:(r + 1) * n, c * LANES:(c + 1) * LANES] = (
                    slab_ref[c, pl.ds(r, n, stride=d), :].astype(BF16))

    permute(DILATIONS[1])
    pg = jnp.dot(hperm_ref[...], wg1_ref[...], preferred_element_type=F32)
    put_slabs(a1_ref, 0, rope_cols(pg, 0, 1, scale))
    put_slabs(a1_ref, n_pairs, pg[:, GROUP_W:])
    kt1 = lax.dot_general(wkg1_ref[...], hperm_ref[...], nt_dims, preferred_element_type=F32)
    write_kt(kt1_ref, kt1, 1)

    permute(DILATIONS[2])
    pg = jnp.dot(hperm_ref[...], wg2_ref[...], preferred_element_type=F32)
    put_slabs(a2_ref, 0, rope_cols(pg, 0, 2, scale))
    k2_a, k2_b = head_masked(rope_cols(pg, GROUP_W, 2, 1.0))
    put_slabs(a2_ref, n_pairs, k2_a)
    put_slabs(a2_ref, 2 * n_pairs, k2_b)
    put_slabs(a2_ref, 3 * n_pairs, pg[:, 2 * GROUP_W:])


def _resident(stacked_shape, layer):
    nd = len(stacked_shape) - 1
    return pl.BlockSpec((None,) + tuple(stacked_shape[1:]), lambda *_: (layer,) + (0,) * nd,
                        pipeline_mode=pl.Buffered(1))


def _ffn_proj(layer, x, g1, wup, wdown, g2, wnat, wknat, wg1, wkg1, wg2, tab, tab_t):
    b = x.shape[0]
    nt = SEQ // TM
    n_pairs = GROUP_W // LANES
    nb_pairs = NA_W // LANES
    row = pl.BlockSpec((None, TM, D_MODEL), lambda i, j: (i, j, 0))
    slabs = lambda n: pl.BlockSpec((None, n, TM, LANES), lambda i, j: (i, 0, j, 0))
    col = lambda h: pl.BlockSpec((None, h, TM), lambda i, j: (i, 0, j))
    slab_shape = lambda n: jax.ShapeDtypeStruct((b, n, SEQ, LANES), BF16)
    out_shape = (
        jax.ShapeDtypeStruct((b, SEQ, D_MODEL), F32),
        slab_shape(2 * n_pairs),
        jax.ShapeDtypeStruct((b, GROUP_W, SEQ), BF16),
        slab_shape(2 * n_pairs),
        jax.ShapeDtypeStruct((b, GROUP_W, SEQ), BF16),
        slab_shape(4 * n_pairs),
        slab_shape(3 * nb_pairs),
        jax.ShapeDtypeStruct((b, NA_W, SEQ), BF16),
    )
    weights = (g1, wup, wdown, g2, wnat, wknat, wg1, wkg1, wg2)
    return pl.pallas_call(
        _ffn_proj_kernel,
        grid=(b, nt),
        in_specs=[row] + [_resident(w.shape, layer) for w in weights]
        + [pl.BlockSpec((TM, tab.shape[1]), lambda i, j: (j, 0)), pl.BlockSpec((tab_t.shape[0], TM), lambda i, j: (0, j))],
        out_specs=(row, slabs(2 * n_pairs), col(GROUP_W), slabs(2 * n_pairs), col(GROUP_W),
                   slabs(4 * n_pairs), slabs(3 * nb_pairs), col(NA_W)),
        out_shape=out_shape,
        scratch_shapes=[pltpu.VMEM((D_MODEL // LANES, TM, LANES), F32), pltpu.VMEM((TM, D_MODEL), BF16)],
        compiler_params=pltpu.CompilerParams(dimension_semantics=("parallel", "parallel"),
                                             vmem_limit_bytes=VMEM_LIMIT_BYTES),
        name="ffn_proj",
    )(x, *weights, tab, tab_t)


def _stacked_scores(q_stack, keys, bias):
    return [jnp.dot(q_stack, keys, preferred_element_type=F32) + bias]


def _split_scores(q, keys, bias):
    out = []
    for h in range(2):
        if isinstance(keys, (tuple, list)):
            s = lax.dot_general(q, keys[h], (((1,), (1,)), ((), ())), preferred_element_type=F32)
        else:
            zeros = jnp.zeros((HEAD_DIM, keys.shape[1]), BF16)
            own = keys[h * HEAD_DIM:(h + 1) * HEAD_DIM]
            s = jnp.dot(q, jnp.concatenate([own, zeros] if h == 0 else [zeros, own], axis=0),
                        preferred_element_type=F32)
        out.append(s + bias)
    return out


def _pair_finish(scores, v):
    vx = jnp.concatenate([v, jnp.ones(v.shape, BF16)], axis=1)
    res = []
    for sc in scores:
        m = jnp.max(sc, axis=-1, keepdims=True)
        p = jnp.exp(sc - m).astype(BF16)
        res.append((jnp.dot(p, vx, preferred_element_type=F32), m))
    if len(res) == 1:
        half = res[0][0].shape[0] // 2
        res = [(res[0][0][h * half:(h + 1) * half], res[0][1][h * half:(h + 1) * half]) for h in range(2)]
    shape = (res[0][0].shape[0], LANES)
    left = lax.broadcasted_iota(jnp.int32, shape, 1) < HEAD_DIM
    acc = jnp.where(left, res[0][0][:, :LANES], res[1][0][:, :LANES])
    l = jnp.where(left, res[0][0][:, LANES:], res[1][0][:, LANES:])
    m = jnp.where(left, jnp.broadcast_to(res[0][1], shape), jnp.broadcast_to(res[1][1], shape))
    return acc, m, l


def _software_pipeline(blocks, depth):
    pending = []
    for score_fn, finish_fn in blocks:
        pending.append((finish_fn, score_fn()))
        if len(pending) > depth:
            fn, sc = pending.pop(0)
            fn(sc)
    for fn, sc in pending:
        fn(sc)


CHUNK = 2 * BAND


def _band_bias(m, n, lo, hi):
    u = lax.broadcasted_iota(jnp.int32, (m, n), 0)
    w = lax.broadcasted_iota(jnp.int32, (m, n), 1)
    d = w - u
    return jnp.where(d >= lo, jnp.where(d <= hi, 0.0, NEG_INF), NEG_INF).astype(F32)


def _dilated_kernel(q0_ref, v0_ref, kt0_ref, q1_ref, v1_ref, kt1_ref, q2_ref, k2a_ref, k2b_ref, v2_ref,
                    o_ref, out_s, lse_s):
    q_refs = (q0_ref, q1_ref, q2_ref)
    v_refs = (v0_ref, v1_ref, v2_ref)

    bias_first = _band_bias(BAND, CHUNK, -BAND, BAND)
    bias_last = _band_bias(BAND, CHUNK, 0, CHUNK)
    bias_full = _band_bias(CHUNK, 2 * CHUNK, 0, CHUNK)
    bias_one = _band_bias(CHUNK, CHUNK, -BAND, BAND)

    def normalise(vals):
        acc, m, l = vals
        return acc / l, m + jnp.log(l)

    def stash(g, outs, vals):
        o, lse = normalise(vals)
        for row0, n, dst, stride in outs:
            idx = pl.ds(dst, n, stride=stride)
            out_s[g - 1, idx, :] = o[row0:row0 + n]
            lse_s[g - 1, idx, :] = lse[row0:row0 + n]

    def merge(g, outs, vals):
        assert g == 0
        o0, lse0 = normalise(vals)
        for row0, n, dst, stride in outs:
            assert stride == 1
            sl = slice(dst, dst + n)
            os_ = [o0[row0:row0 + n], out_s[0, sl, :], out_s[1, sl, :]]
            ls_ = [lse0[row0:row0 + n], lse_s[0, sl, :], lse_s[1, sl, :]]
            mx = jnp.maximum(jnp.maximum(ls_[0], ls_[1]), ls_[2])
            ws = [jnp.exp(x - mx) for x in ls_]
            num = ws[0] * os_[0] + ws[1] * os_[1] + ws[2] * os_[2]
            o_ref[sl, :] = (num / (ws[0] + ws[1] + ws[2])).astype(BF16)

    blocks = []

    def add(g, q_slices, keys_fn, v_slices, bias, outs):
        def score_fn():
            q = jnp.concatenate([q_refs[g][sl, :] for sl in q_slices], axis=0)
            return _split_scores(q, keys_fn(), bias)

        def finish_fn(scores):
            v = jnp.concatenate([v_refs[g][sl, :] for sl in v_slices], axis=0)
            (merge if g == 0 else stash)(g, outs, _pair_finish(scores, v))

        blocks.append((score_fn, finish_fn))

    def class_blocks(g, kt_ref, base, nat, stride):
        nc = len(base)
        rows = lambda j, lo, n: slice(base[j] + lo, base[j] + lo + n)
        add(g, [rows(0, 0, BAND)], lambda: kt_ref[:, rows(0, 0, CHUNK)], [rows(0, 0, CHUNK)],
            bias_first, [(0, BAND, nat[0], stride)])
        for j in range(nc - 1):
            add(g, [rows(j, BAND, BAND), rows(j + 1, 0, BAND)],
                lambda j=j: jnp.concatenate([kt_ref[:, rows(j, 0, CHUNK)], kt_ref[:, rows(j + 1, 0, CHUNK)]], axis=1),
                [rows(j, 0, CHUNK), rows(j + 1, 0, CHUNK)],
                bias_full, [(0, BAND, nat[j] + BAND * stride, stride), (BAND, BAND, nat[j + 1], stride)])
        add(g, [rows(nc - 1, BAND, BAND)], lambda: kt_ref[:, rows(nc - 1, 0, CHUNK)],
            [rows(nc - 1, 0, CHUNK)], bias_last, [(0, BAND, nat[nc - 1] + BAND * stride, stride)])

    nt = SEQ // TM
    d2 = DILATIONS[2]
    piece = TM // d2
    for r in range(d2):
        pieces = [slice(j * TM + r * piece, j * TM + (r + 1) * piece) for j in range(nt)]
        add(2, pieces,
            lambda pieces=pieces: [jnp.concatenate([k_ref[sl, :] for sl in pieces], axis=0) for k_ref in (k2a_ref, k2b_ref)],
            pieces, bias_one, [(j * piece, piece, j * TM + r, d2) for j in range(nt)])

    d1 = DILATIONS[1]
    for r in range(d1):
        class_blocks(1, kt1_ref, [j * TM + r * CHUNK for j in range(nt)], [j * TM + r for j in range(nt)], d1)

    chunk_rows = [c * CHUNK for c in range(SEQ // CHUNK)]
    class_blocks(0, kt0_ref, chunk_rows, chunk_rows, 1)

    _software_pipeline(blocks, DIL_PIPE_DEPTH)


def _dilated(a0, kt0, a1, kt1, a2):
    b = a0.shape[0]
    n_pairs = GROUP_W // LANES

    def slab(off):
        return pl.BlockSpec((None, None, SEQ, LANES), lambda i, p: (i, off + p, 0, 0))

    ktspec = pl.BlockSpec((None, LANES, SEQ), lambda i, p: (i, p, 0))
    return pl.pallas_call(
        _dilated_kernel,
        grid=(b, n_pairs),
        in_specs=[slab(0), slab(n_pairs), ktspec, slab(0), slab(n_pairs), ktspec,
                  slab(0), slab(n_pairs), slab(2 * n_pairs), slab(3 * n_pairs)],
        out_specs=pl.BlockSpec((None, None, SEQ, LANES), lambda i, p: (i, p, 0, 0)),
        out_shape=jax.ShapeDtypeStruct((b, n_pairs, SEQ, LANES), BF16),
        scratch_shapes=[pltpu.VMEM((2, SEQ, LANES), F32), pltpu.VMEM((2, SEQ, LANES), F32)],
        compiler_params=pltpu.CompilerParams(dimension_semantics=("parallel", "parallel"),
                                             vmem_limit_bytes=VMEM_LIMIT_BYTES),
        name="dilated",
    )(a0, a0, kt0, a1, a1, kt1, a2, a2, a2, a2)


def _na_window_start(i):
    return int(np.clip(i - NA_ROWS // 2, 0, GRID_ROWS - NA_ROWS))


def _na_kernel(qa_ref, qb_ref, v_ref, kt_ref, bias_ref, o_ref, kts_ref):
    words = pltpu.bitcast(kt_ref[...], jnp.uint32)
    kts_ref[...] = pltpu.bitcast(pltpu.roll(words, SEQ - GRID_W, 1), BF16)

    blocks = []
    for i in range(GRID_ROWS):
        lo = _na_window_start(i)
        case = NA_CASE_ROWS.index(i) if i in NA_CASE_ROWS else NA_CASE_ROWS.index(NA_ROWS // 2)
        qsl = slice(i * GRID_W, (i + 1) * GRID_W)
        vsl = slice(lo * GRID_W, lo * GRID_W + NA_KEYS)
        if lo % 2 == 0:
            keys_fn = lambda lo=lo: kt_ref[:, lo * GRID_W:lo * GRID_W + NA_KEYS]
        else:
            keys_fn = lambda lo=lo: kts_ref[:, (lo - 1) * GRID_W:(lo - 1) * GRID_W + NA_KEYS]

        def score_fn(qsl=qsl, keys_fn=keys_fn, case=case):
            q_stack = jnp.concatenate([qa_ref[qsl, :], qb_ref[qsl, :]], axis=0)
            return _stacked_scores(q_stack, keys_fn(), bias_ref[case])

        def finish_fn(scores, qsl=qsl, vsl=vsl):
            acc, _, l = _pair_finish(scores, v_ref[vsl, :])
            o_ref[qsl, :] = (acc / l).astype(BF16)

        blocks.append((score_fn, finish_fn))
    _software_pipeline(blocks, NA_PIPE_DEPTH)


def _nbr(layer, nb, ktb, bias_tab):
    b = nb.shape[0]
    n_pairs = NA_W // LANES

    def slab(off):
        return pl.BlockSpec((None, None, SEQ, LANES), lambda p, i: (i, off + p, 0, 0))

    return pl.pallas_call(
        _na_kernel,
        grid=(n_pairs, b),
        in_specs=[slab(0), slab(n_pairs), slab(2 * n_pairs), pl.BlockSpec((None, LANES, SEQ), lambda p, i: (i, p, 0)),
                  pl.BlockSpec((None, None, len(NA_CASE_ROWS), 2 * GRID_W, NA_KEYS),
                               lambda p, i: (layer, p, 0, 0, 0))],
        out_specs=pl.BlockSpec((None, None, SEQ, LANES), lambda p, i: (i, p, 0, 0)),
        out_shape=jax.ShapeDtypeStruct((b, n_pairs, SEQ, LANES), BF16),
        scratch_shapes=[pltpu.VMEM((LANES, SEQ), BF16)],
        compiler_params=pltpu.CompilerParams(dimension_semantics=("parallel", "parallel"),
                                             vmem_limit_bytes=VMEM_LIMIT_BYTES),
        name="nbr",
    )(nb, nb, nb, ktb, bias_tab)


def _na_bias_table(rel_bias):
    n = GRID_W
    n_dr = 2 * NA_ROWS - 1
    nl = rel_bias.shape[0]
    nh = nl * NA_HEADS
    flat = rel_bias.astype(F32).reshape(nh, n_dr, 2 * NA_COLS - 1)
    qc = np.arange(n)
    win_lo = np.clip(qc - NA_COLS // 2, 0, n - NA_COLS)
    col_ok = (qc[None, :] >= win_lo[:, None]) & (qc[None, :] < win_lo[:, None] + NA_COLS)
    offset = np.where(col_ok, qc[None, :] - qc[:, None] + NA_COLS - 1, -1)
    toep = jnp.full((nh, n_dr, n, n), NEG_INF, F32)
    for j in range(2 * NA_COLS - 1):
        toep = jnp.where(offset == j, flat[:, :, j][:, :, None, None], toep)

    cases = []
    for i in NA_CASE_ROWS:
        lo = _na_window_start(i)
        cases.append(jnp.concatenate([toep[:, lo + ki - i + NA_ROWS - 1] for ki in range(NA_ROWS)], axis=-1))
    tab = jnp.stack(cases, axis=1)
    nc = len(NA_CASE_ROWS)
    tab = tab.reshape(nl, NA_HEADS // 2, 2, nc, n, NA_KEYS).transpose(0, 1, 3, 2, 4, 5)
    return tab.reshape(nl, NA_HEADS // 2, nc, 2 * n, NA_KEYS)


def _out_ffn_kernel(x1_ref, ya_ref, yb_ref, gm_ref, wgate_ref, wa_ref, wb_ref, wout_ref,
                    g3_ref, wup_ref, wdown_ref, gf_ref, o_ref, *, final):
    x1 = x1_ref[...]
    h = _rms(x1, gm_ref[...]).astype(BF16)
    gates = jax.nn.sigmoid(jnp.dot(h, wgate_ref[...], preferred_element_type=F32))
    ya = jnp.concatenate([ya_ref[c] for c in range(ya_ref.shape[0])], axis=1)
    yb = jnp.concatenate([yb_ref[c] for c in range(yb_ref.shape[0])], axis=1)
    ba = jnp.dot(ya, wa_ref[...], preferred_element_type=F32)
    bb = jnp.dot(yb, wb_ref[...], preferred_element_type=F32)
    merged = gates[:, :D_MODEL] * ba + gates[:, D_MODEL:] * bb
    x2 = x1 + jnp.dot(merged.astype(BF16), wout_ref[...], preferred_element_type=F32)
    h3 = _rms(x2, g3_ref[...]).astype(BF16)
    x3 = x2 + 0.5 * _swiglu_half(h3, wup_ref, wdown_ref)
    if final:
        x3 = _rms(x3, gf_ref[...])
    o_ref[...] = x3


def _out_ffn(layer, x1, ya, yb, gm, wgate, wa, wb, wout, g3, wup, wdown, gf, final):
    b = x1.shape[0]
    row = pl.BlockSpec((None, TM, D_MODEL), lambda i, j: (i, j, 0))
    slabs = lambda n: pl.BlockSpec((None, n, TM, LANES), lambda i, j: (i, 0, j, 0))
    weights = (gm, wgate, wa, wb, wout, g3, wup, wdown)
    gf_spec = pl.BlockSpec(gf.shape, lambda i, j: (0, 0))
    return pl.pallas_call(
        functools.partial(_out_ffn_kernel, final=final),
        grid=(b, SEQ // TM),
        in_specs=[row, slabs(ya.shape[1]), slabs(yb.shape[1])] + [_resident(w.shape, layer) for w in weights] + [gf_spec],
        out_specs=row,
        out_shape=jax.ShapeDtypeStruct((b, SEQ, D_MODEL), F32),
        compiler_params=pltpu.CompilerParams(dimension_semantics=("parallel", "parallel"),
                                             vmem_limit_bytes=VMEM_LIMIT_BYTES),
        name="out_ffn",
    )(x1, ya, yb, *weights, gf)


def _rope_tables():
    half = HEAD_DIM // 2
    inv_freq = ROPE_THETA ** (-jnp.arange(half, dtype=F32) / half)
    n = np.arange(SEQ)
    slabs, rows = [], []
    for d in DILATIONS:
        tile, local = n // TM, n % TM
        token = tile * TM + (local % (TM // d)) * d + local // (TM // d)
        ang = jnp.asarray(token, F32)[:, None] * inv_freq[None, :]
        cos, sin = jnp.cos(ang), jnp.sin(ang)
        slabs += [jnp.tile(cos, (1, LANES // half)), jnp.tile(jnp.concatenate([-sin, sin], axis=-1), (1, LANES // HEAD_DIM))]
        if d != DILATIONS[2]:
            rows += [cos.T, sin.T]
    return jnp.concatenate(slabs, axis=1), jnp.concatenate(rows, axis=0)


def kernel(x, ffn1_norm, ffn1_w_up, ffn1_w_down, mix_norm, w_in, na_rel_bias, w_branch_a, w_branch_b,
           w_out, ffn2_norm, ffn2_w_up, ffn2_w_down, final_norm):
    assert x.shape[1:] == (SEQ, D_MODEL) and x.dtype == F32
    tab, tab_t = _rope_tables()
    gain = lambda g: g.astype(F32).reshape(-1, 1, D_MODEL)
    w = w_in.astype(BF16)
    aq, ak, av = (w[:, :, i * 3 * GROUP_W:(i + 1) * 3 * GROUP_W] for i in range(3))
    bq, bk, bv = (w[:, :, DIL_QKV + i * NA_W:DIL_QKV + (i + 1) * NA_W] for i in range(3))
    grp = lambda t, g: t[:, :, g * GROUP_W:(g + 1) * GROUP_W]
    wnat = jnp.concatenate([grp(aq, 0), grp(av, 0), bq, bv], axis=2)
    wknat = jnp.concatenate([grp(ak, 0), bk], axis=2).transpose(0, 2, 1)
    wg1 = jnp.concatenate([grp(aq, 1), grp(av, 1)], axis=2)
    wkg1 = grp(ak, 1).transpose(0, 2, 1)
    wg2 = jnp.concatenate([grp(aq, 2), grp(ak, 2), grp(av, 2)], axis=2)
    wgate = w[:, :, DIL_QKV + NA_QKV:]
    g_ffn1, g_mix, g_ffn2 = gain(ffn1_norm), gain(mix_norm), gain(ffn2_norm)
    up1, down1 = ffn1_w_up.astype(BF16), ffn1_w_down.astype(BF16)
    up2, down2 = ffn2_w_up.astype(BF16), ffn2_w_down.astype(BF16)
    wa, wb, wo = w_branch_a.astype(BF16), w_branch_b.astype(BF16), w_out.astype(BF16)
    bias_tab = _na_bias_table(na_rel_bias)
    g_final = final_norm.astype(F32).reshape(1, D_MODEL)

    for layer in range(DEPTH):
        x1, a0, kt0, a1, kt1, a2, nb, ktb = _ffn_proj(
            layer, x, g_ffn1, up1, down1, g_mix, wnat, wknat, wg1, wkg1, wg2, tab, tab_t)
        ya = _dilated(a0, kt0, a1, kt1, a2)
        yb = _nbr(layer, nb, ktb, bias_tab)
        x = _out_ffn(layer, x1, ya, yb, g_mix, wgate, wa, wb, wo, g_ffn2, up2, down2, g_final,
                     final=(layer == DEPTH - 1))
    return x
```

```python
import functools

import numpy as np
import jax
import jax.numpy as jnp
from jax import lax
from jax.experimental import pallas as pl
from jax.experimental.pallas import tpu as pltpu

F32 = jnp.float32
BF16 = jnp.bfloat16

D_MODEL = 1024
SEQ = 2048
DEPTH = 2
HEAD_DIM = 64
DILATIONS = (1, 4, 16)
BAND = 64
GROUP_HEADS = 4
GROUP_W = GROUP_HEADS * HEAD_DIM
NA_HEADS = 8
NA_W = NA_HEADS * HEAD_DIM
GRID_W = 64
GRID_ROWS = SEQ // GRID_W
NA_ROWS = 8
NA_COLS = 16
D_FF = 2816
ROPE_THETA = 10000.0
RMS_EPS = 1e-6
NEG_INF = -1e30
DIL_QKV = 3 * 3 * GROUP_W
NA_QKV = 3 * NA_W

LANES = 128
VMEM_LIMIT_BYTES = 56 * 1024 * 1024

TM = 512
MXU_TILE = 256
SUB_ROWS = 256
FF_CHUNKS = (6 * MXU_TILE, 5 * MXU_TILE)
assert sum(FF_CHUNKS) == D_FF
NA_KEYS = NA_ROWS * GRID_W
DIL_PIPE_DEPTH = 4
NA_PIPE_DEPTH = 4
NA_CASE_ROWS = (0, 1, 2, 3, 4, 29, 30, 31)


def _rms(x, g):
    return x * lax.rsqrt(jnp.mean(x * x, axis=-1, keepdims=True) + RMS_EPS) * g


def _swiglu_half(h, w_up_ref, w_down_ref):
    acc = None
    lo = 0
    for width in FF_CHUNKS:
        g = jnp.dot(h, w_up_ref[:, lo:lo + width], preferred_element_type=F32)
        u = jnp.dot(h, w_up_ref[:, D_FF + lo:D_FF + lo + width], preferred_element_type=F32)
        a = (g * jax.nn.sigmoid(g) * u).astype(BF16)
        part = jnp.dot(a, w_down_ref[lo:lo + width, :], preferred_element_type=F32)
        acc = part if acc is None else acc + part
        lo += width
    return acc


def _rope(t, cos, sin_signed):
    lane = lax.broadcasted_iota(jnp.int32, t.shape, 1)
    first_half = (lane % HEAD_DIM) < (HEAD_DIM // 2)
    partner = jnp.where(first_half, pltpu.roll(t, LANES - HEAD_DIM // 2, 1), pltpu.roll(t, HEAD_DIM // 2, 1))
    return t * cos + partner * sin_signed


def _ffn_proj_kernel(x_ref, g1_ref, wup_ref, wdown_ref, g2_ref, wnat_ref, wknat_ref, wg1_ref, wkg1_ref, wg2_ref,
                     tab_ref, tabt_ref,
                     x1_ref, a0_ref, kt0_ref, a1_ref, kt1_ref, a2_ref, nb_ref, ktb_ref,
                     slab_ref, hperm_ref):
    parts = [slice(i * SUB_ROWS, (i + 1) * SUB_ROWS) for i in range(TM // SUB_ROWS)]
    n_slabs = D_MODEL // LANES
    h2bs = []
    for r in parts:
        x = x_ref[r, :]
        h = _rms(x, g1_ref[...]).astype(BF16)
        x1 = x + 0.5 * _swiglu_half(h, wup_ref, wdown_ref)
        x1_ref[r, :] = x1
        h2 = _rms(x1, g2_ref[...])
        for c in range(n_slabs):
            slab_ref[c, r, :] = h2[:, c * LANES:(c + 1) * LANES]
        h2bs.append(h2.astype(BF16))

    scale = HEAD_DIM ** -0.5
    half = HEAD_DIM // 2
    nt_dims = (((1,), (1,)), ((), ()))

    def tables(order, r):
        cos = tab_ref[r, (2 * order) * LANES:(2 * order + 1) * LANES]
        sin = tab_ref[r, (2 * order + 1) * LANES:(2 * order + 2) * LANES]
        return cos, sin

    def put_slabs(a_ref, first, t, r=slice(None)):
        for c in range(t.shape[1] // LANES):
            a_ref[first + c, r, :] = t[:, c * LANES:(c + 1) * LANES].astype(BF16)

    def head_masked(t):
        lane = lax.broadcasted_iota(jnp.int32, t.shape, 1)
        first = (lane % LANES) < HEAD_DIM
        return jnp.where(first, t, 0.0), jnp.where(first, 0.0, t)

    def rope_cols(p, lo, order, mul, r=slice(None)):
        cos, sin = tables(order, r)
        return jnp.concatenate([_rope(p[:, lo + c * LANES:lo + (c + 1) * LANES] * mul, cos, sin)
                                for c in range(GROUP_W // LANES)], axis=1)

    def write_kt(kt_ref, kt, order, r=slice(None)):
        cos = tabt_ref[(2 * order) * half:(2 * order + 1) * half, r]
        sin = tabt_ref[(2 * order + 1) * half:(2 * order + 2) * half, r]
        for hh in range(GROUP_HEADS):
            t1 = kt[hh * HEAD_DIM:hh * HEAD_DIM + half, :]
            t2 = kt[hh * HEAD_DIM + half:(hh + 1) * HEAD_DIM, :]
            kt_ref[hh * HEAD_DIM:hh * HEAD_DIM + half, r] = (t1 * cos - t2 * sin).astype(BF16)
            kt_ref[hh * HEAD_DIM + half:(hh + 1) * HEAD_DIM, r] = (t2 * cos + t1 * sin).astype(BF16)

    n_pairs = GROUP_W // LANES
    nb_pairs = NA_W // LANES

    for r, h2b in zip(parts, h2bs):
        pn = jnp.dot(h2b, wnat_ref[...], preferred_element_type=F32)
        put_slabs(a0_ref, 0, rope_cols(pn, 0, 0, scale, r), r)
        put_slabs(a0_ref, n_pairs, pn[:, GROUP_W:2 * GROUP_W], r)
        qb_a, qb_b = head_masked(pn[:, 2 * GROUP_W:2 * GROUP_W + NA_W] * scale)
        put_slabs(nb_ref, 0, qb_a, r)
        put_slabs(nb_ref, nb_pairs, qb_b, r)
        put_slabs(nb_ref, 2 * nb_pairs, pn[:, 2 * GROUP_W + NA_W:], r)
        ktn = lax.dot_general(wknat_ref[...], h2b, nt_dims, preferred_element_type=F32)
        write_kt(kt0_ref, ktn[:GROUP_W], 0, r)
        ktb_ref[:, r] = ktn[GROUP_W:].astype(BF16)

    def permute(d):
        n = TM // d
        for r in range(d):
            for c in range(n_slabs):
                hperm_ref[r * n:(r + 1) * n, c * LANES:(c + 1) * LANES] = (
                    slab_ref[c, pl.ds(r, n, stride=d), :].astype(BF16))

    permute(DILATIONS[1])
    pg = jnp.dot(hperm_ref[...], wg1_ref[...], preferred_element_type=F32)
    put_slabs(a1_ref, 0, rope_cols(pg, 0, 1, scale))
    put_slabs(a1_ref, n_pairs, pg[:, GROUP_W:])
    kt1 = lax.dot_general(wkg1_ref[...], hperm_ref[...], nt_dims, preferred_element_type=F32)
    write_kt(kt1_ref, kt1, 1)

    permute(DILATIONS[2])
    pg = jnp.dot(hperm_ref[...], wg2_ref[...], preferred_element_type=F32)
    put_slabs(a2_ref, 0, rope_cols(pg, 0, 2, scale))
    k2_a, k2_b = head_masked(rope_cols(pg, GROUP_W, 2, 1.0))
    put_slabs(a2_ref, n_pairs, k2_a)
    put_slabs(a2_ref, 2 * n_pairs, k2_b)
    put_slabs(a2_ref, 3 * n_pairs, pg[:, 2 * GROUP_W:])


def _resident(stacked_shape, layer):
    nd = len(stacked_shape) - 1
    return pl.BlockSpec((None,) + tuple(stacked_shape[1:]), lambda *_: (layer,) + (0,) * nd,
                        pipeline_mode=pl.Buffered(1))


def _ffn_proj(layer, x, g1, wup, wdown, g2, wnat, wknat, wg1, wkg1, wg2, tab, tab_t):
    b = x.shape[0]
    nt = SEQ // TM
    n_pairs = GROUP_W // LANES
    nb_pairs = NA_W // LANES
    row = pl.BlockSpec((None, TM, D_MODEL), lambda i, j: (i, j, 0))
    slabs = lambda n: pl.BlockSpec((None, n, TM, LANES), lambda i, j: (i, 0, j, 0))
    col = lambda h: pl.BlockSpec((None, h, TM), lambda i, j: (i, 0, j))
    slab_shape = lambda n: jax.ShapeDtypeStruct((b, n, SEQ, LANES), BF16)
    out_shape = (
        jax.ShapeDtypeStruct((b, SEQ, D_MODEL), F32),
        slab_shape(2 * n_pairs),
        jax.ShapeDtypeStruct((b, GROUP_W, SEQ), BF16),
        slab_shape(2 * n_pairs),
        jax.ShapeDtypeStruct((b, GROUP_W, SEQ), BF16),
        slab_shape(4 * n_pairs),
        slab_shape(3 * nb_pairs),
        jax.ShapeDtypeStruct((b, NA_W, SEQ), BF16),
    )
    weights = (g1, wup, wdown, g2, wnat, wknat, wg1, wkg1, wg2)
    return pl.pallas_call(
        _ffn_proj_kernel,
        grid=(b, nt),
        in_specs=[row] + [_resident(w.shape, layer) for w in weights]
        + [pl.BlockSpec((TM, tab.shape[1]), lambda i, j: (j, 0)), pl.BlockSpec((tab_t.shape[0], TM), lambda i, j: (0, j))],
        out_specs=(row, slabs(2 * n_pairs), col(GROUP_W), slabs(2 * n_pairs), col(GROUP_W),
                   slabs(4 * n_pairs), slabs(3 * nb_pairs), col(NA_W)),
        out_shape=out_shape,
        scratch_shapes=[pltpu.VMEM((D_MODEL // LANES, TM, LANES), F32), pltpu.VMEM((TM, D_MODEL), BF16)],
        compiler_params=pltpu.CompilerParams(dimension_semantics=("parallel", "parallel"),
                                             vmem_limit_bytes=VMEM_LIMIT_BYTES),
        name="ffn_proj",
    )(x, *weights, tab, tab_t)


def _stacked_scores(q_stack, keys, bias):
    return [jnp.dot(q_stack, keys, preferred_element_type=F32) + bias]


def _split_scores(q, keys, bias):
    out = []
    for h in range(2):
        if isinstance(keys, (tuple, list)):
            s = lax.dot_general(q, keys[h], (((1,), (1,)), ((), ())), preferred_element_type=F32)
        else:
            zeros = jnp.zeros((HEAD_DIM, keys.shape[1]), BF16)
            own = keys[h * HEAD_DIM:(h + 1) * HEAD_DIM]
            s = jnp.dot(q, jnp.concatenate([own, zeros] if h == 0 else [zeros, own], axis=0),
                        preferred_element_type=F32)
        out.append(s + bias)
    return out


def _pair_finish(scores, v):
    vx = jnp.concatenate([v, jnp.ones(v.shape, BF16)], axis=1)
    res = []
    for sc in scores:
        m = jnp.max(sc, axis=-1, keepdims=True)
        p = jnp.exp(sc - m).astype(BF16)
        res.append((jnp.dot(p, vx, preferred_element_type=F32), m))
    if len(res) == 1:
        half = res[0][0].shape[0] // 2
        res = [(res[0][0][h * half:(h + 1) * half], res[0][1][h * half:(h + 1) * half]) for h in range(2)]
    shape = (res[0][0].shape[0], LANES)
    left = lax.broadcasted_iota(jnp.int32, shape, 1) < HEAD_DIM
    acc = jnp.where(left, res[0][0][:, :LANES], res[1][0][:, :LANES])
    l = jnp.where(left, res[0][0][:, LANES:], res[1][0][:, LANES:])
    m = jnp.where(left, jnp.broadcast_to(res[0][1], shape), jnp.broadcast_to(res[1][1], shape))
    return acc, m, l


def _software_pipeline(blocks, depth):
    pending = []
    for score_fn, finish_fn in blocks:
        pending.append((finish_fn, score_fn()))
        if len(pending) > depth:
            fn, sc = pending.pop(0)
            fn(sc)
    for fn, sc in pending:
        fn(sc)


CHUNK = 2 * BAND


def _band_bias(m, n, lo, hi):
    u = lax.broadcasted_iota(jnp.int32, (m, n), 0)
    w = lax.broadcasted_iota(jnp.int32, (m, n), 1)
    d = w - u
    return jnp.where(d >= lo, jnp.where(d <= hi, 0.0, NEG_INF), NEG_INF).astype(F32)


def _dilated_kernel(q0_ref, v0_ref, kt0_ref, q1_ref, v1_ref, kt1_ref, q2_ref, k2a_ref, k2b_ref, v2_ref,
                    o_ref, out_s, lse_s):
    q_refs = (q0_ref, q1_ref, q2_ref)
    v_refs = (v0_ref, v1_ref, v2_ref)

    bias_first = _band_bias(BAND, CHUNK, -BAND, BAND)
    bias_last = _band_bias(BAND, CHUNK, 0, CHUNK)
    bias_full = _band_bias(CHUNK, 2 * CHUNK, 0, CHUNK)
    bias_one = _band_bias(CHUNK, CHUNK, -BAND, BAND)

    def normalise(vals):
        acc, m, l = vals
        return acc / l, m + jnp.log(l)

    def stash(g, outs, vals):
        o, lse = normalise(vals)
        for row0, n, dst, stride in outs:
            idx = pl.ds(dst, n, stride=stride)
            out_s[g - 1, idx, :] = o[row0:row0 + n]
            lse_s[g - 1, idx, :] = lse[row0:row0 + n]

    def merge(g, outs, vals):
        assert g == 0
        o0, lse0 = normalise(vals)
        for row0, n, dst, stride in outs:
            assert stride == 1
            sl = slice(dst, dst + n)
            os_ = [o0[row0:row0 + n], out_s[0, sl, :], out_s[1, sl, :]]
            ls_ = [lse0[row0:row0 + n], lse_s[0, sl, :], lse_s[1, sl, :]]
            mx = jnp.maximum(jnp.maximum(ls_[0], ls_[1]), ls_[2])
            ws = [jnp.exp(x - mx) for x in ls_]
            num = ws[0] * os_[0] + ws[1] * os_[1] + ws[2] * os_[2]
            o_ref[sl, :] = (num / (ws[0] + ws[1] + ws[2])).astype(BF16)

    blocks = []

    def add(g, q_slices, keys_fn, v_slices, bias, outs):
        def score_fn():
            q = jnp.concatenate([q_refs[g][sl, :] for sl in q_slices], axis=0)
            return _split_scores(q, keys_fn(), bias)

        def finish_fn(scores):
            v = jnp.concatenate([v_refs[g][sl, :] for sl in v_slices], axis=0)
            (merge if g == 0 else stash)(g, outs, _pair_finish(scores, v))

        blocks.append((score_fn, finish_fn))

    def class_blocks(g, kt_ref, base, nat, stride):
        nc = len(base)
        rows = lambda j, lo, n: slice(base[j] + lo, base[j] + lo + n)
        add(g, [rows(0, 0, BAND)], lambda: kt_ref[:, rows(0, 0, CHUNK)], [rows(0, 0, CHUNK)],
            bias_first, [(0, BAND, nat[0], stride)])
        for j in range(nc - 1):
            add(g, [rows(j, BAND, BAND), rows(j + 1, 0, BAND)],
                lambda j=j: jnp.concatenate([kt_ref[:, rows(j, 0, CHUNK)], kt_ref[:, rows(j + 1, 0, CHUNK)]], axis=1),
                [rows(j, 0, CHUNK), rows(j + 1, 0, CHUNK)],
                bias_full, [(0, BAND, nat[j] + BAND * stride, stride), (BAND, BAND, nat[j + 1], stride)])
        add(g, [rows(nc - 1, BAND, BAND)], lambda: kt_ref[:, rows(nc - 1, 0, CHUNK)],
            [rows(nc - 1, 0, CHUNK)], bias_last, [(0, BAND, nat[nc - 1] + BAND * stride, stride)])

    nt = SEQ // TM
    d2 = DILATIONS[2]
    piece = TM // d2
    for r in range(d2):
        pieces = [slice(j * TM + r * piece, j * TM + (r + 1) * piece) for j in range(nt)]
        add(2, pieces,
            lambda pieces=pieces: [jnp.concatenate([k_ref[sl, :] for sl in pieces], axis=0) for k_ref in (k2a_ref, k2b_ref)],
            pieces, bias_one, [(j * piece, piece, j * TM + r, d2) for j in range(nt)])

    d1 = DILATIONS[1]
    for r in range(d1):
        class_blocks(1, kt1_ref, [j * TM + r * CHUNK for j in range(nt)], [j * TM + r for j in range(nt)], d1)

    chunk_rows = [c * CHUNK for c in range(SEQ // CHUNK)]
    class_blocks(0, kt0_ref, chunk_rows, chunk_rows, 1)

    _software_pipeline(blocks, DIL_PIPE_DEPTH)


def _dilated(a0, kt0, a1, kt1, a2):
    b = a0.shape[0]
    n_pairs = GROUP_W // LANES

    def slab(off):
        return pl.BlockSpec((None, None, SEQ, LANES), lambda i, p: (i, off + p, 0, 0))

    ktspec = pl.BlockSpec((None, LANES, SEQ), lambda i, p: (i, p, 0))
    return pl.pallas_call(
        _dilated_kernel,
        grid=(b, n_pairs),
        in_specs=[slab(0), slab(n_pairs), ktspec, slab(0), slab(n_pairs), ktspec,
                  slab(0), slab(n_pairs), slab(2 * n_pairs), slab(3 * n_pairs)],
        out_specs=pl.BlockSpec((None, None, SEQ, LANES), lambda i, p: (i, p, 0, 0)),
        out_shape=jax.ShapeDtypeStruct((b, n_pairs, SEQ, LANES), BF16),
        scratch_shapes=[pltpu.VMEM((2, SEQ, LANES), F32), pltpu.VMEM((2, SEQ, LANES), F32)],
        compiler_params=pltpu.CompilerParams(dimension_semantics=("parallel", "parallel"),
                                             vmem_limit_bytes=VMEM_LIMIT_BYTES),
        name="dilated",
    )(a0, a0, kt0, a1, a1, kt1, a2, a2, a2, a2)


def _na_window_start(i):
    return int(np.clip(i - NA_ROWS // 2, 0, GRID_ROWS - NA_ROWS))


def _na_kernel(qa_ref, qb_ref, v_ref, kt_ref, bias_ref, o_ref, kts_ref):
    words = pltpu.bitcast(kt_ref[...], jnp.uint32)
    kts_ref[...] = pltpu.bitcast(pltpu.roll(words, SEQ - GRID_W, 1), BF16)

    blocks = []
    for i in range(GRID_ROWS):
        lo = _na_window_start(i)
        case = NA_CASE_ROWS.index(i) if i in NA_CASE_ROWS else NA_CASE_ROWS.index(NA_ROWS // 2)
        qsl = slice(i * GRID_W, (i + 1) * GRID_W)
        vsl = slice(lo * GRID_W, lo * GRID_W + NA_KEYS)
        if lo % 2 == 0:
            keys_fn = lambda lo=lo: kt_ref[:, lo * GRID_W:lo * GRID_W + NA_KEYS]
        else:
            keys_fn = lambda lo=lo: kts_ref[:, (lo - 1) * GRID_W:(lo - 1) * GRID_W + NA_KEYS]

        def score_fn(qsl=qsl, keys_fn=keys_fn, case=case):
            q_stack = jnp.concatenate([qa_ref[qsl, :], qb_ref[qsl, :]], axis=0)
            return _stacked_scores(q_stack, keys_fn(), bias_ref[case])

        def finish_fn(scores, qsl=qsl, vsl=vsl):
            acc, _, l = _pair_finish(scores, v_ref[vsl, :])
            o_ref[qsl, :] = (acc / l).astype(BF16)

        blocks.append((score_fn, finish_fn))
    _software_pipeline(blocks, NA_PIPE_DEPTH)


def _nbr(layer, nb, ktb, bias_tab):
    b = nb.shape[0]
    n_pairs = NA_W // LANES

    def slab(off):
        return pl.BlockSpec((None, None, SEQ, LANES), lambda p, i: (i, off + p, 0, 0))

    return pl.pallas_call(
        _na_kernel,
        grid=(n_pairs, b),
        in_specs=[slab(0), slab(n_pairs), slab(2 * n_pairs), pl.BlockSpec((None, LANES, SEQ), lambda p, i: (i, p, 0)),
                  pl.BlockSpec((None, None, len(NA_CASE_ROWS), 2 * GRID_W, NA_KEYS),
                               lambda p, i: (layer, p, 0, 0, 0))],
        out_specs=pl.BlockSpec((None, None, SEQ, LANES), lambda p, i: (i, p, 0, 0)),
        out_shape=jax.ShapeDtypeStruct((b, n_pairs, SEQ, LANES), BF16),
        scratch_shapes=[pltpu.VMEM((LANES, SEQ), BF16)],
        compiler_params=pltpu.CompilerParams(dimension_semantics=("parallel", "parallel"),
                                             vmem_limit_bytes=VMEM_LIMIT_BYTES),
        name="nbr",
    )(nb, nb, nb, ktb, bias_tab)


def _na_bias_table(rel_bias):
    n = GRID_W
    n_dr = 2 * NA_ROWS - 1
    nl = rel_bias.shape[0]
    nh = nl * NA_HEADS
    pad_lo = (n - 1) - (NA_COLS - 1)
    flat = rel_bias.astype(F32).reshape(nh, n_dr, 2 * NA_COLS - 1)
    v = jnp.pad(flat, ((0, 0), (0, 0), (pad_lo, 2 * n - (2 * NA_COLS - 1) - pad_lo)))
    skew = jnp.broadcast_to(v[:, :, None, :], (nh, n_dr, n, 2 * n)).reshape(nh, n_dr, 2 * n * n)
    skew = skew[:, :, :n * (2 * n - 1)].reshape(nh, n_dr, n, 2 * n - 1)
    qc = np.arange(n)
    win_lo = np.clip(qc - NA_COLS // 2, 0, n - NA_COLS)
    col_ok = (qc[None, :] >= win_lo[:, None]) & (qc[None, :] < win_lo[:, None] + NA_COLS)
    toep = jnp.where(col_ok, skew[:, :, :, n - 1:], NEG_INF)

    cases = []
    for i in NA_CASE_ROWS:
        first = _na_window_start(i) - i + NA_ROWS - 1
        window = toep[:, first:first + NA_ROWS]
        cases.append(window.transpose(0, 2, 1, 3).reshape(nh, n, NA_KEYS))
    tab = jnp.stack(cases, axis=1)
    nc = len(NA_CASE_ROWS)
    tab = tab.reshape(nl, NA_HEADS // 2, 2, nc, n, NA_KEYS).transpose(0, 1, 3, 2, 4, 5)
    return tab.reshape(nl, NA_HEADS // 2, nc, 2 * n, NA_KEYS)


def _out_ffn_kernel(x1_ref, ya_ref, yb_ref, gm_ref, wgate_ref, wa_ref, wb_ref, wout_ref,
                    g3_ref, wup_ref, wdown_ref, gf_ref, o_ref, *, final):
    n_split = TM // SUB_ROWS
    parts = [slice(i * SUB_ROWS, (i + 1) * SUB_ROWS) for i in range(n_split)]
    x1s, merged = [], []
    for r in parts:
        x1 = x1_ref[r, :]
        h = _rms(x1, gm_ref[...]).astype(BF16)
        gates = jax.nn.sigmoid(jnp.dot(h, wgate_ref[...], preferred_element_type=F32))
        ya = jnp.concatenate([ya_ref[c, r, :] for c in range(ya_ref.shape[0])], axis=1)
        yb = jnp.concatenate([yb_ref[c, r, :] for c in range(yb_ref.shape[0])], axis=1)
        ba = jnp.dot(ya, wa_ref[...], preferred_element_type=F32)
        bb = jnp.dot(yb, wb_ref[...], preferred_element_type=F32)
        x1s.append(x1)
        merged.append((gates[:, :D_MODEL] * ba + gates[:, D_MODEL:] * bb).astype(BF16))
    x2s = [x1 + jnp.dot(m, wout_ref[...], preferred_element_type=F32) for x1, m in zip(x1s, merged)]
    h3s = [_rms(x2, g3_ref[...]).astype(BF16) for x2 in x2s]
    for r, x2, h3 in zip(parts, x2s, h3s):
        x3 = x2 + 0.5 * _swiglu_half(h3, wup_ref, wdown_ref)
        if final:
            x3 = _rms(x3, gf_ref[...])
        o_ref[r, :] = x3


def _out_ffn(layer, x1, ya, yb, gm, wgate, wa, wb, wout, g3, wup, wdown, gf, final):
    b = x1.shape[0]
    row = pl.BlockSpec((None, TM, D_MODEL), lambda i, j: (i, j, 0))
    slabs = lambda n: pl.BlockSpec((None, n, TM, LANES), lambda i, j: (i, 0, j, 0))
    weights = (gm, wgate, wa, wb, wout, g3, wup, wdown)
    gf_spec = pl.BlockSpec(gf.shape, lambda i, j: (0, 0))
    return pl.pallas_call(
        functools.partial(_out_ffn_kernel, final=final),
        grid=(b, SEQ // TM),
        in_specs=[row, slabs(ya.shape[1]), slabs(yb.shape[1])] + [_resident(w.shape, layer) for w in weights] + [gf_spec],
        out_specs=row,
        out_shape=jax.ShapeDtypeStruct((b, SEQ, D_MODEL), F32),
        compiler_params=pltpu.CompilerParams(dimension_semantics=("parallel", "parallel"),
                                             vmem_limit_bytes=VMEM_LIMIT_BYTES),
        name="out_ffn",
    )(x1, ya, yb, *weights, gf)


def _rope_tables():
    half = HEAD_DIM // 2
    inv_freq = ROPE_THETA ** (-jnp.arange(half, dtype=F32) / half)
    n = np.arange(SEQ)
    tile, local = n // TM, n % TM
    tokens = np.stack([tile * TM + (local % (TM // d)) * d + local // (TM // d) for d in DILATIONS], axis=1)
    ang = jnp.asarray(tokens, F32)[:, :, None] * inv_freq[None, None, :]
    cos, sin = jnp.cos(ang), jnp.sin(ang)
    cos_slab = jnp.tile(cos, (1, 1, LANES // half))
    sin_slab = jnp.tile(jnp.concatenate([-sin, sin], axis=-1), (1, 1, LANES // HEAD_DIM))
    tab = jnp.stack([cos_slab, sin_slab], axis=2).reshape(SEQ, 2 * len(DILATIONS) * LANES)
    tab_t = jnp.stack([cos[:, :2], sin[:, :2]], axis=2).reshape(SEQ, 4 * half).T
    return tab, tab_t


def kernel(x, ffn1_norm, ffn1_w_up, ffn1_w_down, mix_norm, w_in, na_rel_bias, w_branch_a, w_branch_b,
           w_out, ffn2_norm, ffn2_w_up, ffn2_w_down, final_norm):
    assert x.shape[1:] == (SEQ, D_MODEL) and x.dtype == F32
    tab, tab_t = _rope_tables()
    gain = lambda g: g.astype(F32).reshape(-1, 1, D_MODEL)
    w = w_in
    aq, ak, av = (w[:, :, i * 3 * GROUP_W:(i + 1) * 3 * GROUP_W] for i in range(3))
    bq, bk, bv = (w[:, :, DIL_QKV + i * NA_W:DIL_QKV + (i + 1) * NA_W] for i in range(3))
    grp = lambda t, g: t[:, :, g * GROUP_W:(g + 1) * GROUP_W]
    wnat = jnp.concatenate([grp(aq, 0), grp(av, 0), bq, bv], axis=2).astype(BF16)
    wknat = jnp.concatenate([grp(ak, 0), bk], axis=2).transpose(0, 2, 1).astype(BF16)
    wg1 = jnp.concatenate([grp(aq, 1), grp(av, 1)], axis=2).astype(BF16)
    wkg1 = grp(ak, 1).transpose(0, 2, 1).astype(BF16)
    wg2 = jnp.concatenate([grp(aq, 2), grp(ak, 2), grp(av, 2)], axis=2).astype(BF16)
    wgate = w[:, :, DIL_QKV + NA_QKV:].astype(BF16)
    g_ffn1, g_mix, g_ffn2 = gain(ffn1_norm), gain(mix_norm), gain(ffn2_norm)
    up1, down1 = ffn1_w_up.astype(BF16), ffn1_w_down.astype(BF16)
    up2, down2 = ffn2_w_up.astype(BF16), ffn2_w_down.astype(BF16)
    wa, wb, wo = w_branch_a.astype(BF16), w_branch_b.astype(BF16), w_out.astype(BF16)
    bias_tab = _na_bias_table(na_rel_bias)
    g_final = final_norm.astype(F32).reshape(1, D_MODEL)

    for layer in range(DEPTH):
        x1, a0, kt0, a1, kt1, a2, nb, ktb = _ffn_proj(
            layer, x, g_ffn1, up1, down1, g_mix, wnat, wknat, wg1, wkg1, wg2, tab, tab_t)
        ya = _dilated(a0, kt0, a1, kt1, a2)
        yb = _nbr(layer, nb, ktb, bias_tab)
        x = _out_ffn(layer, x1, ya, yb, g_mix, wgate, wa, wb, wo, g_ffn2, up2, down2, g_final,
                     final=(layer == DEPTH - 1))
    return x
```

```python
import functools

import numpy as np
import jax
import jax.numpy as jnp
from jax import lax
from jax.experimental import pallas as pl
from jax.experimental.pallas import tpu as pltpu

F32 = jnp.float32
BF16 = jnp.bfloat16

D_MODEL = 1024
SEQ = 2048
DEPTH = 2
HEAD_DIM = 64
DILATIONS = (1, 4, 16)
BAND = 64
GROUP_HEADS = 4
GROUP_W = GROUP_HEADS * HEAD_DIM
NA_HEADS = 8
NA_W = NA_HEADS * HEAD_DIM
GRID_W = 64
GRID_ROWS = SEQ // GRID_W
NA_ROWS = 8
NA_COLS = 16
D_FF = 2816
ROPE_THETA = 10000.0
RMS_EPS = 1e-6
NEG_INF = -1e30
DIL_QKV = 3 * 3 * GROUP_W
NA_QKV = 3 * NA_W

LANES = 128
VMEM_LIMIT_BYTES = 56 * 1024 * 1024

TM = 512
MXU_TILE = 256
SUB_ROWS = 256
FF_CHUNKS = (6 * MXU_TILE, 5 * MXU_TILE)
assert sum(FF_CHUNKS) == D_FF
NA_KEYS = NA_ROWS * GRID_W
DIL_PIPE_DEPTH = 4
NA_PIPE_DEPTH = 4
NA_CASE_ROWS = (0, 1, 2, 3, 4, 29, 30, 31)


def _rms(x, g):
    return x * lax.rsqrt(jnp.mean(x * x, axis=-1, keepdims=True) + RMS_EPS) * g


def _swiglu_half(h, w_up_ref, w_down_ref):
    acc = None
    lo = 0
    for width in FF_CHUNKS:
        g = jnp.dot(h, w_up_ref[:, lo:lo + width], preferred_element_type=F32)
        u = jnp.dot(h, w_up_ref[:, D_FF + lo:D_FF + lo + width], preferred_element_type=F32)
        a = (g * jax.nn.sigmoid(g) * u).astype(BF16)
        part = jnp.dot(a, w_down_ref[lo:lo + width, :], preferred_element_type=F32)
        acc = part if acc is None else acc + part
        lo += width
    return acc


def _rope(t, cos, sin_signed):
    lane = lax.broadcasted_iota(jnp.int32, t.shape, 1)
    first_half = (lane % HEAD_DIM) < (HEAD_DIM // 2)
    partner = jnp.where(first_half, pltpu.roll(t, LANES - HEAD_DIM // 2, 1), pltpu.roll(t, HEAD_DIM // 2, 1))
    return t * cos + partner * sin_signed


def _ffn_proj_kernel(x_ref, g1_ref, wup_ref, wdown_ref, g2_ref, wnat_ref, wknat_ref, wg1_ref, wkg1_ref, wg2_ref,
                     tab_ref, tabt_ref,
                     x1_ref, a0_ref, kt0_ref, a1_ref, kt1_ref, a2_ref, nb_ref, ktb_ref,
                     slab_ref, hperm_ref):
    parts = [slice(i * SUB_ROWS, (i + 1) * SUB_ROWS) for i in range(TM // SUB_ROWS)]
    n_slabs = D_MODEL // LANES
    h2bs = []
    for r in parts:
        x = x_ref[r, :]
        h = _rms(x, g1_ref[...]).astype(BF16)
        x1 = x + 0.5 * _swiglu_half(h, wup_ref, wdown_ref)
        x1_ref[r, :] = x1
        h2 = _rms(x1, g2_ref[...])
        for c in range(n_slabs):
            slab_ref[c, r, :] = h2[:, c * LANES:(c + 1) * LANES]
        h2bs.append(h2.astype(BF16))

    scale = HEAD_DIM ** -0.5
    half = HEAD_DIM // 2
    nt_dims = (((1,), (1,)), ((), ()))

    def tables(order, r):
        cos = tab_ref[r, (2 * order) * LANES:(2 * order + 1) * LANES]
        sin = tab_ref[r, (2 * order + 1) * LANES:(2 * order + 2) * LANES]
        return cos, sin

    def put_slabs(a_ref, first, t, r=slice(None)):
        for c in range(t.shape[1] // LANES):
            a_ref[first + c, r, :] = t[:, c * LANES:(c + 1) * LANES].astype(BF16)

    def head_masked(t):
        lane = lax.broadcasted_iota(jnp.int32, t.shape, 1)
        first = (lane % LANES) < HEAD_DIM
        return jnp.where(first, t, 0.0), jnp.where(first, 0.0, t)

    def rope_cols(p, lo, order, mul, r=slice(None)):
        cos, sin = tables(order, r)
        return jnp.concatenate([_rope(p[:, lo + c * LANES:lo + (c + 1) * LANES] * mul, cos, sin)
                                for c in range(GROUP_W // LANES)], axis=1)

    def write_kt(kt_ref, kt, order, r=slice(None)):
        cos = tabt_ref[(2 * order) * half:(2 * order + 1) * half, r]
        sin = tabt_ref[(2 * order + 1) * half:(2 * order + 2) * half, r]
        for hh in range(GROUP_HEADS):
            t1 = kt[hh * HEAD_DIM:hh * HEAD_DIM + half, :]
            t2 = kt[hh * HEAD_DIM + half:(hh + 1) * HEAD_DIM, :]
            kt_ref[hh * HEAD_DIM:hh * HEAD_DIM + half, r] = (t1 * cos - t2 * sin).astype(BF16)
            kt_ref[hh * HEAD_DIM + half:(hh + 1) * HEAD_DIM, r] = (t2 * cos + t1 * sin).astype(BF16)

    n_pairs = GROUP_W // LANES
    nb_pairs = NA_W // LANES

    for r, h2b in zip(parts, h2bs):
        pn = jnp.dot(h2b, wnat_ref[...], preferred_element_type=F32)
        put_slabs(a0_ref, 0, rope_cols(pn, 0, 0, scale, r), r)
        put_slabs(a0_ref, n_pairs, pn[:, GROUP_W:2 * GROUP_W], r)
        qb_a, qb_b = head_masked(pn[:, 2 * GROUP_W:2 * GROUP_W + NA_W] * scale)
        put_slabs(nb_ref, 0, qb_a, r)
        put_slabs(nb_ref, nb_pairs, qb_b, r)
        put_slabs(nb_ref, 2 * nb_pairs, pn[:, 2 * GROUP_W + NA_W:], r)
        ktn = lax.dot_general(wknat_ref[...], h2b, nt_dims, preferred_element_type=F32)
        write_kt(kt0_ref, ktn[:GROUP_W], 0, r)
        ktb_ref[:, r] = ktn[GROUP_W:].astype(BF16)

    def permute(d):
        n = TM // d
        for r in range(d):
            for c in range(n_slabs):
                hperm_ref[r * n:(r + 1) * n, c * LANES:(c + 1) * LANES] = (
                    slab_ref[c, pl.ds(r, n, stride=d), :].astype(BF16))

    permute(DILATIONS[1])
    pg = jnp.dot(hperm_ref[...], wg1_ref[...], preferred_element_type=F32)
    put_slabs(a1_ref, 0, rope_cols(pg, 0, 1, scale))
    put_slabs(a1_ref, n_pairs, pg[:, GROUP_W:])
    kt1 = lax.dot_general(wkg1_ref[...], hperm_ref[...], nt_dims, preferred_element_type=F32)
    write_kt(kt1_ref, kt1, 1)

    permute(DILATIONS[2])
    pg = jnp.dot(hperm_ref[...], wg2_ref[...], preferred_element_type=F32)
    put_slabs(a2_ref, 0, rope_cols(pg, 0, 2, scale))
    k2_a, k2_b = head_masked(rope_cols(pg, GROUP_W, 2, 1.0))
    put_slabs(a2_ref, n_pairs, k2_a)
    put_slabs(a2_ref, 2 * n_pairs, k2_b)
    put_slabs(a2_ref, 3 * n_pairs, pg[:, 2 * GROUP_W:])


def _resident(stacked_shape, layer):
    nd = len(stacked_shape) - 1
    return pl.BlockSpec((None,) + tuple(stacked_shape[1:]), lambda *_: (layer,) + (0,) * nd,
                        pipeline_mode=pl.Buffered(1))


def _ffn_proj(layer, x, g1, wup, wdown, g2, wnat, wknat, wg1, wkg1, wg2, tab, tab_t):
    b = x.shape[0]
    nt = SEQ // TM
    n_pairs = GROUP_W // LANES
    nb_pairs = NA_W // LANES
    row = pl.BlockSpec((None, TM, D_MODEL), lambda i, j: (i, j, 0))
    slabs = lambda n: pl.BlockSpec((None, n, TM, LANES), lambda i, j: (i, 0, j, 0))
    col = lambda h: pl.BlockSpec((None, h, TM), lambda i, j: (i, 0, j))
    slab_shape = lambda n: jax.ShapeDtypeStruct((b, n, SEQ, LANES), BF16)
    out_shape = (
        jax.ShapeDtypeStruct((b, SEQ, D_MODEL), F32),
        slab_shape(2 * n_pairs),
        jax.ShapeDtypeStruct((b, GROUP_W, SEQ), BF16),
        slab_shape(2 * n_pairs),
        jax.ShapeDtypeStruct((b, GROUP_W, SEQ), BF16),
        slab_shape(4 * n_pairs),
        slab_shape(3 * nb_pairs),
        jax.ShapeDtypeStruct((b, NA_W, SEQ), BF16),
    )
    weights = (g1, wup, wdown, g2, wnat, wknat, wg1, wkg1, wg2)
    return pl.pallas_call(
        _ffn_proj_kernel,
        grid=(b, nt),
        in_specs=[row] + [_resident(w.shape, layer) for w in weights]
        + [pl.BlockSpec((TM, tab.shape[1]), lambda i, j: (j, 0)), pl.BlockSpec((tab_t.shape[0], TM), lambda i, j: (0, j))],
        out_specs=(row, slabs(2 * n_pairs), col(GROUP_W), slabs(2 * n_pairs), col(GROUP_W),
                   slabs(4 * n_pairs), slabs(3 * nb_pairs), col(NA_W)),
        out_shape=out_shape,
        scratch_shapes=[pltpu.VMEM((D_MODEL // LANES, TM, LANES), F32), pltpu.VMEM((TM, D_MODEL), BF16)],
        compiler_params=pltpu.CompilerParams(dimension_semantics=("parallel", "parallel"),
                                             vmem_limit_bytes=VMEM_LIMIT_BYTES),
        name="ffn_proj",
    )(x, *weights, tab, tab_t)


def _stacked_scores(q_stack, keys, bias):
    return [jnp.dot(q_stack, keys, preferred_element_type=F32) + bias]


def _split_scores(q, keys, bias):
    out = []
    for h in range(2):
        if isinstance(keys, (tuple, list)):
            s = lax.dot_general(q, keys[h], (((1,), (1,)), ((), ())), preferred_element_type=F32)
        else:
            zeros = jnp.zeros((HEAD_DIM, keys.shape[1]), BF16)
            own = keys[h * HEAD_DIM:(h + 1) * HEAD_DIM]
            s = jnp.dot(q, jnp.concatenate([own, zeros] if h == 0 else [zeros, own], axis=0),
                        preferred_element_type=F32)
        out.append(s + bias)
    return out


def _pair_finish(scores, v):
    vx = jnp.concatenate([v, jnp.ones(v.shape, BF16)], axis=1)
    res = []
    for sc in scores:
        m = jnp.max(sc, axis=-1, keepdims=True)
        p = jnp.exp(sc - m).astype(BF16)
        res.append((jnp.dot(p, vx, preferred_element_type=F32), m))
    if len(res) == 1:
        half = res[0][0].shape[0] // 2
        res = [(res[0][0][h * half:(h + 1) * half], res[0][1][h * half:(h + 1) * half]) for h in range(2)]
    shape = (res[0][0].shape[0], LANES)
    left = lax.broadcasted_iota(jnp.int32, shape, 1) < HEAD_DIM
    acc = jnp.where(left, res[0][0][:, :LANES], res[1][0][:, :LANES])
    l = jnp.where(left, res[0][0][:, LANES:], res[1][0][:, LANES:])
    m = jnp.where(left, jnp.broadcast_to(res[0][1], shape), jnp.broadcast_to(res[1][1], shape))
    return acc, m, l


def _software_pipeline(blocks, depth):
    pending = []
    for score_fn, finish_fn in blocks:
        pending.append((finish_fn, score_fn()))
        if len(pending) > depth:
            fn, sc = pending.pop(0)
            fn(sc)
    for fn, sc in pending:
        fn(sc)


CHUNK = 2 * BAND


def _band_bias(m, n, lo, hi):
    u = lax.broadcasted_iota(jnp.int32, (m, n), 0)
    w = lax.broadcasted_iota(jnp.int32, (m, n), 1)
    d = w - u
    return jnp.where(d >= lo, jnp.where(d <= hi, 0.0, NEG_INF), NEG_INF).astype(F32)


def _dilated_kernel(q0_ref, v0_ref, kt0_ref, q1_ref, v1_ref, kt1_ref, q2_ref, k2a_ref, k2b_ref, v2_ref,
                    o_ref, out_s, lse_s):
    q_refs = (q0_ref, q1_ref, q2_ref)
    v_refs = (v0_ref, v1_ref, v2_ref)

    bias_first = _band_bias(BAND, CHUNK, -BAND, BAND)
    bias_last = _band_bias(BAND, CHUNK, 0, CHUNK)
    bias_full = _band_bias(CHUNK, 2 * CHUNK, 0, CHUNK)
    bias_one = _band_bias(CHUNK, CHUNK, -BAND, BAND)

    def normalise(vals):
        acc, m, l = vals
        return acc / l, m + jnp.log(l)

    def stash(g, outs, vals):
        o, lse = normalise(vals)
        for row0, n, dst, stride in outs:
            idx = pl.ds(dst, n, stride=stride)
            out_s[g - 1, idx, :] = o[row0:row0 + n]
            lse_s[g - 1, idx, :] = lse[row0:row0 + n]

    def merge(g, outs, vals):
        assert g == 0
        o0, lse0 = normalise(vals)
        for row0, n, dst, stride in outs:
            assert stride == 1
            sl = slice(dst, dst + n)
            os_ = [o0[row0:row0 + n], out_s[0, sl, :], out_s[1, sl, :]]
            ls_ = [lse0[row0:row0 + n], lse_s[0, sl, :], lse_s[1, sl, :]]
            mx = jnp.maximum(jnp.maximum(ls_[0], ls_[1]), ls_[2])
            ws = [jnp.exp(x - mx) for x in ls_]
            num = ws[0] * os_[0] + ws[1] * os_[1] + ws[2] * os_[2]
            o_ref[sl, :] = (num / (ws[0] + ws[1] + ws[2])).astype(BF16)

    blocks = []

    def add(g, q_slices, keys_fn, v_slices, bias, outs):
        def score_fn():
            q = jnp.concatenate([q_refs[g][sl, :] for sl in q_slices], axis=0)
            return _split_scores(q, keys_fn(), bias)

        def finish_fn(scores):
            v = jnp.concatenate([v_refs[g][sl, :] for sl in v_slices], axis=0)
            (merge if g == 0 else stash)(g, outs, _pair_finish(scores, v))

        blocks.append((score_fn, finish_fn))

    def class_blocks(g, kt_ref, base, nat, stride):
        nc = len(base)
        rows = lambda j, lo, n: slice(base[j] + lo, base[j] + lo + n)
        add(g, [rows(0, 0, BAND)], lambda: kt_ref[:, rows(0, 0, CHUNK)], [rows(0, 0, CHUNK)],
            bias_first, [(0, BAND, nat[0], stride)])
        for j in range(nc - 1):
            add(g, [rows(j, BAND, BAND), rows(j + 1, 0, BAND)],
                lambda j=j: jnp.concatenate([kt_ref[:, rows(j, 0, CHUNK)], kt_ref[:, rows(j + 1, 0, CHUNK)]], axis=1),
                [rows(j, 0, CHUNK), rows(j + 1, 0, CHUNK)],
                bias_full, [(0, BAND, nat[j] + BAND * stride, stride), (BAND, BAND, nat[j + 1], stride)])
        add(g, [rows(nc - 1, BAND, BAND)], lambda: kt_ref[:, rows(nc - 1, 0, CHUNK)],
            [rows(nc - 1, 0, CHUNK)], bias_last, [(0, BAND, nat[nc - 1] + BAND * stride, stride)])

    nt = SEQ // TM
    d2 = DILATIONS[2]
    piece = TM // d2
    for r in range(d2):
        pieces = [slice(j * TM + r * piece, j * TM + (r + 1) * piece) for j in range(nt)]
        add(2, pieces,
            lambda pieces=pieces: [jnp.concatenate([k_ref[sl, :] for sl in pieces], axis=0) for k_ref in (k2a_ref, k2b_ref)],
            pieces, bias_one, [(j * piece, piece, j * TM + r, d2) for j in range(nt)])

    d1 = DILATIONS[1]
    for r in range(d1):
        class_blocks(1, kt1_ref, [j * TM + r * CHUNK for j in range(nt)], [j * TM + r for j in range(nt)], d1)

    chunk_rows = [c * CHUNK for c in range(SEQ // CHUNK)]
    class_blocks(0, kt0_ref, chunk_rows, chunk_rows, 1)

    _software_pipeline(blocks, DIL_PIPE_DEPTH)


def _dilated(a0, kt0, a1, kt1, a2):
    b = a0.shape[0]
    n_pairs = GROUP_W // LANES

    def slab(off):
        return pl.BlockSpec((None, None, SEQ, LANES), lambda i, p: (i, off + p, 0, 0))

    ktspec = pl.BlockSpec((None, LANES, SEQ), lambda i, p: (i, p, 0))
    return pl.pallas_call(
        _dilated_kernel,
        grid=(b, n_pairs),
        in_specs=[slab(0), slab(n_pairs), ktspec, slab(0), slab(n_pairs), ktspec,
                  slab(0), slab(n_pairs), slab(2 * n_pairs), slab(3 * n_pairs)],
        out_specs=pl.BlockSpec((None, None, SEQ, LANES), lambda i, p: (i, p, 0, 0)),
        out_shape=jax.ShapeDtypeStruct((b, n_pairs, SEQ, LANES), BF16),
        scratch_shapes=[pltpu.VMEM((2, SEQ, LANES), F32), pltpu.VMEM((2, SEQ, LANES), F32)],
        compiler_params=pltpu.CompilerParams(dimension_semantics=("parallel", "parallel"),
                                             vmem_limit_bytes=VMEM_LIMIT_BYTES),
        name="dilated",
    )(a0, a0, kt0, a1, a1, kt1, a2, a2, a2, a2)


def _na_window_start(i):
    return int(np.clip(i - NA_ROWS // 2, 0, GRID_ROWS - NA_ROWS))


def _na_kernel(qa_ref, qb_ref, v_ref, kt_ref, bias_ref, o_ref, kts_ref):
    words = pltpu.bitcast(kt_ref[...], jnp.uint32)
    kts_ref[...] = pltpu.bitcast(pltpu.roll(words, SEQ - GRID_W, 1), BF16)

    blocks = []
    for i in range(GRID_ROWS):
        lo = _na_window_start(i)
        case = NA_CASE_ROWS.index(i) if i in NA_CASE_ROWS else NA_CASE_ROWS.index(NA_ROWS // 2)
        qsl = slice(i * GRID_W, (i + 1) * GRID_W)
        vsl = slice(lo * GRID_W, lo * GRID_W + NA_KEYS)
        if lo % 2 == 0:
            keys_fn = lambda lo=lo: kt_ref[:, lo * GRID_W:lo * GRID_W + NA_KEYS]
        else:
            keys_fn = lambda lo=lo: kts_ref[:, (lo - 1) * GRID_W:(lo - 1) * GRID_W + NA_KEYS]

        def score_fn(qsl=qsl, keys_fn=keys_fn, case=case):
            q_stack = jnp.concatenate([qa_ref[qsl, :], qb_ref[qsl, :]], axis=0)
            return _stacked_scores(q_stack, keys_fn(), bias_ref[case])

        def finish_fn(scores, qsl=qsl, vsl=vsl):
            acc, _, l = _pair_finish(scores, v_ref[vsl, :])
            o_ref[qsl, :] = (acc / l).astype(BF16)

        blocks.append((score_fn, finish_fn))
    _software_pipeline(blocks, NA_PIPE_DEPTH)


def _nbr(layer, nb, ktb, bias_tab):
    b = nb.shape[0]
    n_pairs = NA_W // LANES

    def slab(off):
        return pl.BlockSpec((None, None, SEQ, LANES), lambda p, i: (i, off + p, 0, 0))

    return pl.pallas_call(
        _na_kernel,
        grid=(n_pairs, b),
        in_specs=[slab(0), slab(n_pairs), slab(2 * n_pairs), pl.BlockSpec((None, LANES, SEQ), lambda p, i: (i, p, 0)),
                  pl.BlockSpec((None, None, len(NA_CASE_ROWS), 2 * GRID_W, NA_KEYS),
                               lambda p, i: (layer, p, 0, 0, 0))],
        out_specs=pl.BlockSpec((None, None, SEQ, LANES), lambda p, i: (i, p, 0, 0)),
        out_shape=jax.ShapeDtypeStruct((b, n_pairs, SEQ, LANES), BF16),
        scratch_shapes=[pltpu.VMEM((LANES, SEQ), BF16)],
        compiler_params=pltpu.CompilerParams(dimension_semantics=("parallel", "parallel"),
                                             vmem_limit_bytes=VMEM_LIMIT_BYTES),
        name="nbr",
    )(nb, nb, nb, ktb, bias_tab)


def _bias_table_kernel(rb_ref, o_ref):
    n = GRID_W
    qc = lax.broadcasted_iota(jnp.int32, (n, LANES), 0)
    lane = lax.broadcasted_iota(jnp.int32, (n, LANES), 1)
    kc = lane % n
    win_lo = jnp.clip(qc - NA_COLS // 2, 0, n - NA_COLS)
    windowed = lambda t: jnp.where(kc >= win_lo, jnp.where(kc < win_lo + NA_COLS, t, NEG_INF), NEG_INF)
    low = lane < n
    placed = []
    for i in range(2 * NA_ROWS - 1):
        v = jnp.broadcast_to(rb_ref[i:i + 1, :], (n, LANES))
        even = pltpu.roll(v, n + 1, 1, stride=1, stride_axis=0)
        odd = pltpu.roll(v, 1, 1, stride=1, stride_axis=0)
        placed.append((windowed(even), windowed(odd)))
    for c, i in enumerate(NA_CASE_ROWS):
        first = _na_window_start(i) - i + NA_ROWS - 1
        for t in range(NA_ROWS // 2):
            o_ref[c, :, t * LANES:(t + 1) * LANES] = jnp.where(low, placed[first + 2 * t][0],
                                                               placed[first + 2 * t + 1][1])


def _na_bias_table(rel_bias):
    nl = rel_bias.shape[0]
    n_dr = 2 * NA_ROWS - 1
    nc = len(NA_CASE_ROWS)
    pad_lo = (GRID_W - 1) - (NA_COLS - 1)
    rows = jnp.pad(rel_bias.astype(F32), ((0, 0), (0, 0), (0, 0), (pad_lo, LANES - (2 * NA_COLS - 1) - pad_lo)))
    return pl.pallas_call(
        _bias_table_kernel,
        grid=(nl, NA_HEADS),
        in_specs=[pl.BlockSpec((None, None, n_dr, LANES), lambda l, h: (l, h, 0, 0))],
        out_specs=pl.BlockSpec((None, None, nc, GRID_W, NA_KEYS), lambda l, h: (l, h // 2, 0, h % 2, 0)),
        out_shape=jax.ShapeDtypeStruct((nl, NA_HEADS // 2, nc, 2 * GRID_W, NA_KEYS), F32),
        compiler_params=pltpu.CompilerParams(dimension_semantics=("parallel", "parallel")),
        name="bias_table",
    )(rows)


def _out_ffn_kernel(x1_ref, ya_ref, yb_ref, gm_ref, wgate_ref, wa_ref, wb_ref, wout_ref,
                    g3_ref, wup_ref, wdown_ref, gf_ref, o_ref, *, final):
    n_split = TM // SUB_ROWS
    parts = [slice(i * SUB_ROWS, (i + 1) * SUB_ROWS) for i in range(n_split)]
    x1s, merged = [], []
    for r in parts:
        x1 = x1_ref[r, :]
        h = _rms(x1, gm_ref[...]).astype(BF16)
        gates = jax.nn.sigmoid(jnp.dot(h, wgate_ref[...], preferred_element_type=F32))
        ya = jnp.concatenate([ya_ref[c, r, :] for c in range(ya_ref.shape[0])], axis=1)
        yb = jnp.concatenate([yb_ref[c, r, :] for c in range(yb_ref.shape[0])], axis=1)
        ba = jnp.dot(ya, wa_ref[...], preferred_element_type=F32)
        bb = jnp.dot(yb, wb_ref[...], preferred_element_type=F32)
        x1s.append(x1)
        merged.append((gates[:, :D_MODEL] * ba + gates[:, D_MODEL:] * bb).astype(BF16))
    x2s = [x1 + jnp.dot(m, wout_ref[...], preferred_element_type=F32) for x1, m in zip(x1s, merged)]
    h3s = [_rms(x2, g3_ref[...]).astype(BF16) for x2 in x2s]
    for r, x2, h3 in zip(parts, x2s, h3s):
        x3 = x2 + 0.5 * _swiglu_half(h3, wup_ref, wdown_ref)
        if final:
            x3 = _rms(x3, gf_ref[...])
        o_ref[r, :] = x3


def _out_ffn(layer, x1, ya, yb, gm, wgate, wa, wb, wout, g3, wup, wdown, gf, final):
    b = x1.shape[0]
    row = pl.BlockSpec((None, TM, D_MODEL), lambda i, j: (i, j, 0))
    slabs = lambda n: pl.BlockSpec((None, n, TM, LANES), lambda i, j: (i, 0, j, 0))
    weights = (gm, wgate, wa, wb, wout, g3, wup, wdown)
    gf_spec = pl.BlockSpec(gf.shape, lambda i, j: (0, 0))
    return pl.pallas_call(
        functools.partial(_out_ffn_kernel, final=final),
        grid=(b, SEQ // TM),
        in_specs=[row, slabs(ya.shape[1]), slabs(yb.shape[1])] + [_resident(w.shape, layer) for w in weights] + [gf_spec],
        out_specs=row,
        out_shape=jax.ShapeDtypeStruct((b, SEQ, D_MODEL), F32),
        compiler_params=pltpu.CompilerParams(dimension_semantics=("parallel", "parallel"),
                                             vmem_limit_bytes=VMEM_LIMIT_BYTES),
        name="out_ffn",
    )(x1, ya, yb, *weights, gf)


def _rope_tables():
    half = HEAD_DIM // 2
    pos = jnp.arange(SEQ)
    inv_freq = ROPE_THETA ** (-jnp.arange(half, dtype=F32) / half)
    ang = pos.astype(F32)[:, None] * inv_freq[None, :]
    cos = jnp.cos(ang)
    sin = jnp.sin(ang)
    cos_slab = jnp.tile(cos, (1, LANES // half))
    sin_slab = jnp.tile(jnp.concatenate([-sin, sin], axis=-1), (1, LANES // HEAD_DIM))

    def class_major(t, d):
        w = t.shape[-1]
        return t.reshape(SEQ // TM, TM // d, d, w).transpose(0, 2, 1, 3).reshape(SEQ, w)

    slabs = [cos_slab, sin_slab]
    for d in DILATIONS[1:]:
        slabs += [class_major(cos_slab, d), class_major(sin_slab, d)]
    d1 = DILATIONS[1]
    rows = [cos.T, sin.T, class_major(cos, d1).T, class_major(sin, d1).T]
    return jnp.concatenate(slabs, axis=1), jnp.concatenate(rows, axis=0)


def kernel(x, ffn1_norm, ffn1_w_up, ffn1_w_down, mix_norm, w_in, na_rel_bias, w_branch_a, w_branch_b,
           w_out, ffn2_norm, ffn2_w_up, ffn2_w_down, final_norm):
    assert x.shape[1:] == (SEQ, D_MODEL) and x.dtype == F32
    tab, tab_t = _rope_tables()
    gain = lambda g: g.astype(F32).reshape(-1, 1, D_MODEL)
    w = w_in.astype(BF16)
    aq, ak, av = (w[:, :, i * 3 * GROUP_W:(i + 1) * 3 * GROUP_W] for i in range(3))
    bq, bk, bv = (w[:, :, DIL_QKV + i * NA_W:DIL_QKV + (i + 1) * NA_W] for i in range(3))
    grp = lambda t, g: t[:, :, g * GROUP_W:(g + 1) * GROUP_W]
    wnat = jnp.concatenate([grp(aq, 0), grp(av, 0), bq, bv], axis=2)
    wknat = jnp.concatenate([grp(ak, 0), bk], axis=2).transpose(0, 2, 1)
    wg1 = jnp.concatenate([grp(aq, 1), grp(av, 1)], axis=2)
    wkg1 = grp(ak, 1).transpose(0, 2, 1)
    wg2 = jnp.concatenate([grp(aq, 2), grp(ak, 2), grp(av, 2)], axis=2)
    wgate = w[:, :, DIL_QKV + NA_QKV:]
    g_ffn1, g_mix, g_ffn2 = gain(ffn1_norm), gain(mix_norm), gain(ffn2_norm)
    up1, down1 = ffn1_w_up.astype(BF16), ffn1_w_down.astype(BF16)
    up2, down2 = ffn2_w_up.astype(BF16), ffn2_w_down.astype(BF16)
    wa, wb, wo = w_branch_a.astype(BF16), w_branch_b.astype(BF16), w_out.astype(BF16)
    bias_tab = _na_bias_table(na_rel_bias)
    g_final = final_norm.astype(F32).reshape(1, D_MODEL)

    for layer in range(DEPTH):
        x1, a0, kt0, a1, kt1, a2, nb, ktb = _ffn_proj(
            layer, x, g_ffn1, up1, down1, g_mix, wnat, wknat, wg1, wkg1, wg2, tab, tab_t)
        ya = _dilated(a0, kt0, a1, kt1, a2)
        yb = _nbr(layer, nb, ktb, bias_tab)
        x = _out_ffn(layer, x1, ya, yb, g_mix, wgate, wa, wb, wo, g_ffn2, up2, down2, g_final,
                     final=(layer == DEPTH - 1))
    return x
```

```python
import functools

import numpy as np
import jax
import jax.numpy as jnp
from jax import lax
from jax.experimental import pallas as pl
from jax.experimental.pallas import tpu as pltpu

F32 = jnp.float32
BF16 = jnp.bfloat16

D_MODEL = 1024
SEQ = 2048
DEPTH = 2
HEAD_DIM = 64
DILATIONS = (1, 4, 16)
BAND = 64
GROUP_HEADS = 4
GROUP_W = GROUP_HEADS * HEAD_DIM
NA_HEADS = 8
NA_W = NA_HEADS * HEAD_DIM
GRID_W = 64
GRID_ROWS = SEQ // GRID_W
NA_ROWS = 8
NA_COLS = 16
D_FF = 2816
ROPE_THETA = 10000.0
RMS_EPS = 1e-6
NEG_INF = -1e30
DIL_QKV = 3 * 3 * GROUP_W
NA_QKV = 3 * NA_W

LANES = 128
VMEM_LIMIT_BYTES = 56 * 1024 * 1024

TM = 512
MXU_TILE = 256
SUB_ROWS = 256
FF_CHUNKS = (6 * MXU_TILE, 5 * MXU_TILE)
assert sum(FF_CHUNKS) == D_FF
NA_KEYS = NA_ROWS * GRID_W
DIL_PIPE_DEPTH = 4
NA_PIPE_DEPTH = 3
NA_CASE_ROWS = (0, 1, 2, 3, 4, 29, 30, 31)


def _rms(x, g):
    return x * lax.rsqrt(jnp.mean(x * x, axis=-1, keepdims=True) + RMS_EPS) * g


def _swiglu_half(h, w_up_ref, w_down_ref):
    acc = None
    lo = 0
    for width in FF_CHUNKS:
        g = jnp.dot(h, w_up_ref[:, lo:lo + width], preferred_element_type=F32)
        u = jnp.dot(h, w_up_ref[:, D_FF + lo:D_FF + lo + width], preferred_element_type=F32)
        a = (g * jax.nn.sigmoid(g) * u).astype(BF16)
        part = jnp.dot(a, w_down_ref[lo:lo + width, :], preferred_element_type=F32)
        acc = part if acc is None else acc + part
        lo += width
    return acc


def _rope(t, cos, sin_signed):
    lane = lax.broadcasted_iota(jnp.int32, t.shape, 1)
    first_half = (lane % HEAD_DIM) < (HEAD_DIM // 2)
    partner = jnp.where(first_half, pltpu.roll(t, LANES - HEAD_DIM // 2, 1), pltpu.roll(t, HEAD_DIM // 2, 1))
    return t * cos + partner * sin_signed


def _ffn_proj_kernel(x_ref, g1_ref, wup_ref, wdown_ref, g2_ref, wnat_ref, wknat_ref, wg1_ref, wkg1_ref, wg2_ref,
                     tab_ref, tabt_ref,
                     x1_ref, a0_ref, kt0_ref, a1_ref, kt1_ref, a2_ref, nb_ref, ktb_ref,
                     slab_ref, hperm_ref):
    parts = [slice(i * SUB_ROWS, (i + 1) * SUB_ROWS) for i in range(TM // SUB_ROWS)]
    n_slabs = D_MODEL // LANES
    h2bs = []
    for r in parts:
        x = x_ref[r, :]
        h = _rms(x, g1_ref[...]).astype(BF16)
        x1 = x + 0.5 * _swiglu_half(h, wup_ref, wdown_ref)
        x1_ref[r, :] = x1
        h2 = _rms(x1, g2_ref[...])
        for c in range(n_slabs):
            slab_ref[c, r, :] = h2[:, c * LANES:(c + 1) * LANES]
        h2bs.append(h2.astype(BF16))

    scale = HEAD_DIM ** -0.5
    half = HEAD_DIM // 2
    nt_dims = (((1,), (1,)), ((), ()))

    def tables(order, r):
        cos = tab_ref[r, (2 * order) * LANES:(2 * order + 1) * LANES]
        sin = tab_ref[r, (2 * order + 1) * LANES:(2 * order + 2) * LANES]
        return cos, sin

    def put_slabs(a_ref, first, t, r=slice(None)):
        for c in range(t.shape[1] // LANES):
            a_ref[first + c, r, :] = t[:, c * LANES:(c + 1) * LANES].astype(BF16)

    def head_masked(t):
        lane = lax.broadcasted_iota(jnp.int32, t.shape, 1)
        first = (lane % LANES) < HEAD_DIM
        return jnp.where(first, t, 0.0), jnp.where(first, 0.0, t)

    def rope_cols(p, lo, order, mul, r=slice(None)):
        cos, sin = tables(order, r)
        return jnp.concatenate([_rope(p[:, lo + c * LANES:lo + (c + 1) * LANES] * mul, cos, sin)
                                for c in range(GROUP_W // LANES)], axis=1)

    def write_kt(kt_ref, kt, order, r=slice(None)):
        cos = tabt_ref[(2 * order) * half:(2 * order + 1) * half, r]
        sin = tabt_ref[(2 * order + 1) * half:(2 * order + 2) * half, r]
        for hh in range(GROUP_HEADS):
            t1 = kt[hh * HEAD_DIM:hh * HEAD_DIM + half, :]
            t2 = kt[hh * HEAD_DIM + half:(hh + 1) * HEAD_DIM, :]
            kt_ref[hh * HEAD_DIM:hh * HEAD_DIM + half, r] = (t1 * cos - t2 * sin).astype(BF16)
            kt_ref[hh * HEAD_DIM + half:(hh + 1) * HEAD_DIM, r] = (t2 * cos + t1 * sin).astype(BF16)

    n_pairs = GROUP_W // LANES
    nb_pairs = NA_W // LANES

    for r, h2b in zip(parts, h2bs):
        pn = jnp.dot(h2b, wnat_ref[...], preferred_element_type=F32)
        put_slabs(a0_ref, 0, rope_cols(pn, 0, 0, scale, r), r)
        put_slabs(a0_ref, n_pairs, pn[:, GROUP_W:2 * GROUP_W], r)
        qb_a, qb_b = head_masked(pn[:, 2 * GROUP_W:2 * GROUP_W + NA_W] * scale)
        put_slabs(nb_ref, 0, qb_a, r)
        put_slabs(nb_ref, nb_pairs, qb_b, r)
        put_slabs(nb_ref, 2 * nb_pairs, pn[:, 2 * GROUP_W + NA_W:], r)
        ktn = lax.dot_general(wknat_ref[...], h2b, nt_dims, preferred_element_type=F32)
        write_kt(kt0_ref, ktn[:GROUP_W], 0, r)
        ktb_ref[:, r] = ktn[GROUP_W:].astype(BF16)

    def permute(d):
        n = TM // d
        for r in range(d):
            for c in range(n_slabs):
                hperm_ref[r * n:(r + 1) * n, c * LANES:(c + 1) * LANES] = (
                    slab_ref[c, pl.ds(r, n, stride=d), :].astype(BF16))

    permute(DILATIONS[1])
    pg = jnp.dot(hperm_ref[...], wg1_ref[...], preferred_element_type=F32)
    put_slabs(a1_ref, 0, rope_cols(pg, 0, 1, scale))
    put_slabs(a1_ref, n_pairs, pg[:, GROUP_W:])
    kt1 = lax.dot_general(wkg1_ref[...], hperm_ref[...], nt_dims, preferred_element_type=F32)
    write_kt(kt1_ref, kt1, 1)

    permute(DILATIONS[2])
    pg = jnp.dot(hperm_ref[...], wg2_ref[...], preferred_element_type=F32)
    put_slabs(a2_ref, 0, rope_cols(pg, 0, 2, scale))
    k2_a, k2_b = head_masked(rope_cols(pg, GROUP_W, 2, 1.0))
    put_slabs(a2_ref, n_pairs, k2_a)
    put_slabs(a2_ref, 2 * n_pairs, k2_b)
    put_slabs(a2_ref, 3 * n_pairs, pg[:, 2 * GROUP_W:])


def _resident(stacked_shape, layer):
    nd = len(stacked_shape) - 1
    return pl.BlockSpec((None,) + tuple(stacked_shape[1:]), lambda *_: (layer,) + (0,) * nd,
                        pipeline_mode=pl.Buffered(1))


def _ffn_proj(layer, x, g1, wup, wdown, g2, wnat, wknat, wg1, wkg1, wg2, tab, tab_t):
    b = x.shape[0]
    nt = SEQ // TM
    n_pairs = GROUP_W // LANES
    nb_pairs = NA_W // LANES
    row = pl.BlockSpec((None, TM, D_MODEL), lambda i, j: (i, j, 0))
    slabs = lambda n: pl.BlockSpec((None, n, TM, LANES), lambda i, j: (i, 0, j, 0))
    col = lambda h: pl.BlockSpec((None, h, TM), lambda i, j: (i, 0, j))
    slab_shape = lambda n: jax.ShapeDtypeStruct((b, n, SEQ, LANES), BF16)
    out_shape = (
        jax.ShapeDtypeStruct((b, SEQ, D_MODEL), F32),
        slab_shape(2 * n_pairs),
        jax.ShapeDtypeStruct((b, GROUP_W, SEQ), BF16),
        slab_shape(2 * n_pairs),
        jax.ShapeDtypeStruct((b, GROUP_W, SEQ), BF16),
        slab_shape(4 * n_pairs),
        slab_shape(3 * nb_pairs),
        jax.ShapeDtypeStruct((b, NA_W, SEQ), BF16),
    )
    weights = (g1, wup, wdown, g2, wnat, wknat, wg1, wkg1, wg2)
    return pl.pallas_call(
        _ffn_proj_kernel,
        grid=(b, nt),
        in_specs=[row] + [_resident(w.shape, layer) for w in weights]
        + [pl.BlockSpec((TM, tab.shape[1]), lambda i, j: (j, 0)), pl.BlockSpec((tab_t.shape[0], TM), lambda i, j: (0, j))],
        out_specs=(row, slabs(2 * n_pairs), col(GROUP_W), slabs(2 * n_pairs), col(GROUP_W),
                   slabs(4 * n_pairs), slabs(3 * nb_pairs), col(NA_W)),
        out_shape=out_shape,
        scratch_shapes=[pltpu.VMEM((D_MODEL // LANES, TM, LANES), F32), pltpu.VMEM((TM, D_MODEL), BF16)],
        compiler_params=pltpu.CompilerParams(dimension_semantics=("parallel", "parallel"),
                                             vmem_limit_bytes=VMEM_LIMIT_BYTES),
        name="ffn_proj",
    )(x, *weights, tab, tab_t)


def _stacked_scores(q_stack, keys, bias):
    return [jnp.dot(q_stack, keys, preferred_element_type=F32) + bias]


def _split_scores(q, keys, bias):
    out = []
    for h in range(2):
        if isinstance(keys, (tuple, list)):
            s = lax.dot_general(q, keys[h], (((1,), (1,)), ((), ())), preferred_element_type=F32)
        else:
            zeros = jnp.zeros((HEAD_DIM, keys.shape[1]), BF16)
            own = keys[h * HEAD_DIM:(h + 1) * HEAD_DIM]
            s = jnp.dot(q, jnp.concatenate([own, zeros] if h == 0 else [zeros, own], axis=0),
                        preferred_element_type=F32)
        out.append(s + bias)
    return out


def _pair_finish(scores, v):
    vx = jnp.concatenate([v, jnp.ones(v.shape, BF16)], axis=1)
    res = []
    for sc in scores:
        m = jnp.max(sc, axis=-1, keepdims=True)
        p = jnp.exp(sc - m).astype(BF16)
        res.append((jnp.dot(p, vx, preferred_element_type=F32), m))
    if len(res) == 1:
        half = res[0][0].shape[0] // 2
        res = [(res[0][0][h * half:(h + 1) * half], res[0][1][h * half:(h + 1) * half]) for h in range(2)]
    shape = (res[0][0].shape[0], LANES)
    left = lax.broadcasted_iota(jnp.int32, shape, 1) < HEAD_DIM
    acc = jnp.where(left, res[0][0][:, :LANES], res[1][0][:, :LANES])
    l = jnp.where(left, res[0][0][:, LANES:], res[1][0][:, LANES:])
    m = jnp.where(left, jnp.broadcast_to(res[0][1], shape), jnp.broadcast_to(res[1][1], shape))
    return acc, m, l


def _software_pipeline(blocks, depth):
    pending = []
    for score_fn, finish_fn in blocks:
        pending.append((finish_fn, score_fn()))
        if len(pending) > depth:
            fn, sc = pending.pop(0)
            fn(sc)
    for fn, sc in pending:
        fn(sc)


CHUNK = 2 * BAND


def _band_bias(m, n, lo, hi):
    u = lax.broadcasted_iota(jnp.int32, (m, n), 0)
    w = lax.broadcasted_iota(jnp.int32, (m, n), 1)
    d = w - u
    return jnp.where(d >= lo, jnp.where(d <= hi, 0.0, NEG_INF), NEG_INF).astype(F32)


def _dilated_kernel(a0_ref, kt0_full, a1_ref, kt1_full, a2_ref, o_full, out_full, lse_full):
    n_pairs = GROUP_W // LANES
    blocks = []
    for p in range(n_pairs):
        _dilated_pair(blocks, a0_ref.at[p], a0_ref.at[n_pairs + p], kt0_full.at[p * LANES:(p + 1) * LANES],
                      a1_ref.at[p], a1_ref.at[n_pairs + p], kt1_full.at[p * LANES:(p + 1) * LANES],
                      a2_ref.at[p], a2_ref.at[n_pairs + p], a2_ref.at[2 * n_pairs + p], a2_ref.at[3 * n_pairs + p],
                      o_full.at[p], out_full.at[p], lse_full.at[p])
    _software_pipeline(blocks, DIL_PIPE_DEPTH)


def _dilated_pair(blocks, q0_ref, v0_ref, kt0_ref, q1_ref, v1_ref, kt1_ref, q2_ref, k2a_ref, k2b_ref, v2_ref,
                  o_ref, out_s, lse_s):
    q_refs = (q0_ref, q1_ref, q2_ref)
    v_refs = (v0_ref, v1_ref, v2_ref)

    bias_first = _band_bias(BAND, CHUNK, -BAND, BAND)
    bias_last = _band_bias(BAND, CHUNK, 0, CHUNK)
    bias_full = _band_bias(CHUNK, 2 * CHUNK, 0, CHUNK)
    bias_one = _band_bias(CHUNK, CHUNK, -BAND, BAND)

    def normalise(vals):
        acc, m, l = vals
        return acc / l, m + jnp.log(l)

    def stash(g, outs, vals):
        o, lse = normalise(vals)
        for row0, n, dst, stride in outs:
            idx = pl.ds(dst, n, stride=stride)
            out_s[g - 1, idx, :] = o[row0:row0 + n]
            lse_s[g - 1, idx, :] = lse[row0:row0 + n]

    def merge(g, outs, vals):
        assert g == 0
        o0, lse0 = normalise(vals)
        for row0, n, dst, stride in outs:
            assert stride == 1
            sl = slice(dst, dst + n)
            os_ = [o0[row0:row0 + n], out_s[0, sl, :], out_s[1, sl, :]]
            ls_ = [lse0[row0:row0 + n], lse_s[0, sl, :], lse_s[1, sl, :]]
            mx = jnp.maximum(jnp.maximum(ls_[0], ls_[1]), ls_[2])
            ws = [jnp.exp(x - mx) for x in ls_]
            num = ws[0] * os_[0] + ws[1] * os_[1] + ws[2] * os_[2]
            o_ref[sl, :] = (num / (ws[0] + ws[1] + ws[2])).astype(BF16)

    def add(g, q_slices, keys_fn, v_slices, bias, outs):
        def score_fn():
            q = jnp.concatenate([q_refs[g][sl, :] for sl in q_slices], axis=0)
            return _split_scores(q, keys_fn(), bias)

        def finish_fn(scores):
            v = jnp.concatenate([v_refs[g][sl, :] for sl in v_slices], axis=0)
            (merge if g == 0 else stash)(g, outs, _pair_finish(scores, v))

        blocks.append((score_fn, finish_fn))

    def class_blocks(g, kt_ref, base, nat, stride):
        nc = len(base)
        rows = lambda j, lo, n: slice(base[j] + lo, base[j] + lo + n)
        add(g, [rows(0, 0, BAND)], lambda: kt_ref[:, rows(0, 0, CHUNK)], [rows(0, 0, CHUNK)],
            bias_first, [(0, BAND, nat[0], stride)])
        for j in range(nc - 1):
            add(g, [rows(j, BAND, BAND), rows(j + 1, 0, BAND)],
                lambda j=j: jnp.concatenate([kt_ref[:, rows(j, 0, CHUNK)], kt_ref[:, rows(j + 1, 0, CHUNK)]], axis=1),
                [rows(j, 0, CHUNK), rows(j + 1, 0, CHUNK)],
                bias_full, [(0, BAND, nat[j] + BAND * stride, stride), (BAND, BAND, nat[j + 1], stride)])
        add(g, [rows(nc - 1, BAND, BAND)], lambda: kt_ref[:, rows(nc - 1, 0, CHUNK)],
            [rows(nc - 1, 0, CHUNK)], bias_last, [(0, BAND, nat[nc - 1] + BAND * stride, stride)])

    nt = SEQ // TM
    d2 = DILATIONS[2]
    piece = TM // d2
    for r in range(d2):
        pieces = [slice(j * TM + r * piece, j * TM + (r + 1) * piece) for j in range(nt)]
        add(2, pieces,
            lambda pieces=pieces: [jnp.concatenate([k_ref[sl, :] for sl in pieces], axis=0) for k_ref in (k2a_ref, k2b_ref)],
            pieces, bias_one, [(j * piece, piece, j * TM + r, d2) for j in range(nt)])

    d1 = DILATIONS[1]
    for r in range(d1):
        class_blocks(1, kt1_ref, [j * TM + r * CHUNK for j in range(nt)], [j * TM + r for j in range(nt)], d1)

    chunk_rows = [c * CHUNK for c in range(SEQ // CHUNK)]
    class_blocks(0, kt0_ref, chunk_rows, chunk_rows, 1)


def _dilated(a0, kt0, a1, kt1, a2):
    b = a0.shape[0]
    n_pairs = GROUP_W // LANES
    slabs = lambda n: pl.BlockSpec((None, n, SEQ, LANES), lambda i: (i, 0, 0, 0))
    ktspec = pl.BlockSpec((None, GROUP_W, SEQ), lambda i: (i, 0, 0))
    return pl.pallas_call(
        _dilated_kernel,
        grid=(b,),
        in_specs=[slabs(2 * n_pairs), ktspec, slabs(2 * n_pairs), ktspec, slabs(4 * n_pairs)],
        out_specs=slabs(n_pairs),
        out_shape=jax.ShapeDtypeStruct((b, n_pairs, SEQ, LANES), BF16),
        scratch_shapes=[pltpu.VMEM((n_pairs, 2, SEQ, LANES), F32), pltpu.VMEM((n_pairs, 2, SEQ, LANES), F32)],
        compiler_params=pltpu.CompilerParams(dimension_semantics=("parallel",),
                                             vmem_limit_bytes=VMEM_LIMIT_BYTES),
        name="dilated",
    )(a0, kt0, a1, kt1, a2)


def _na_window_start(i):
    return int(np.clip(i - NA_ROWS // 2, 0, GRID_ROWS - NA_ROWS))


def _na_kernel(qa_ref, qb_ref, v_ref, kt_ref, bias_ref, o_ref, kts_ref):
    words = pltpu.bitcast(kt_ref[...], jnp.uint32)
    kts_ref[...] = pltpu.bitcast(pltpu.roll(words, SEQ - GRID_W, 1), BF16)

    blocks = []
    for i in range(GRID_ROWS):
        lo = _na_window_start(i)
        case = NA_CASE_ROWS.index(i) if i in NA_CASE_ROWS else NA_CASE_ROWS.index(NA_ROWS // 2)
        qsl = slice(i * GRID_W, (i + 1) * GRID_W)
        vsl = slice(lo * GRID_W, lo * GRID_W + NA_KEYS)
        if lo % 2 == 0:
            keys_fn = lambda lo=lo: kt_ref[:, lo * GRID_W:lo * GRID_W + NA_KEYS]
        else:
            keys_fn = lambda lo=lo: kts_ref[:, (lo - 1) * GRID_W:(lo - 1) * GRID_W + NA_KEYS]

        def score_fn(qsl=qsl, keys_fn=keys_fn, case=case):
            q_stack = jnp.concatenate([qa_ref[qsl, :], qb_ref[qsl, :]], axis=0)
            return _stacked_scores(q_stack, keys_fn(), bias_ref[case])

        def finish_fn(scores, qsl=qsl, vsl=vsl):
            acc, _, l = _pair_finish(scores, v_ref[vsl, :])
            o_ref[qsl, :] = (acc / l).astype(BF16)

        blocks.append((score_fn, finish_fn))
    _software_pipeline(blocks, NA_PIPE_DEPTH)


def _nbr(layer, nb, ktb, bias_tab):
    b = nb.shape[0]
    n_pairs = NA_W // LANES

    def slab(off):
        return pl.BlockSpec((None, None, SEQ, LANES), lambda p, i: (i, off + p, 0, 0))

    return pl.pallas_call(
        _na_kernel,
        grid=(n_pairs, b),
        in_specs=[slab(0), slab(n_pairs), slab(2 * n_pairs), pl.BlockSpec((None, LANES, SEQ), lambda p, i: (i, p, 0)),
                  pl.BlockSpec((None, None, len(NA_CASE_ROWS), 2 * GRID_W, NA_KEYS),
                               lambda p, i: (layer, p, 0, 0, 0))],
        out_specs=pl.BlockSpec((None, None, SEQ, LANES), lambda p, i: (i, p, 0, 0)),
        out_shape=jax.ShapeDtypeStruct((b, n_pairs, SEQ, LANES), BF16),
        scratch_shapes=[pltpu.VMEM((LANES, SEQ), BF16)],
        compiler_params=pltpu.CompilerParams(dimension_semantics=("parallel", "parallel"),
                                             vmem_limit_bytes=VMEM_LIMIT_BYTES),
        name="nbr",
    )(nb, nb, nb, ktb, bias_tab)


def _bias_table_kernel(rb_ref, o_ref):
    n = GRID_W
    qc = lax.broadcasted_iota(jnp.int32, (n, LANES), 0)
    lane = lax.broadcasted_iota(jnp.int32, (n, LANES), 1)
    kc = lane % n
    win_lo = jnp.clip(qc - NA_COLS // 2, 0, n - NA_COLS)
    windowed = lambda t: jnp.where(kc >= win_lo, jnp.where(kc < win_lo + NA_COLS, t, NEG_INF), NEG_INF)
    low = lane < n
    placed = []
    for i in range(2 * NA_ROWS - 1):
        v = jnp.broadcast_to(rb_ref[i:i + 1, :], (n, LANES))
        even = pltpu.roll(v, n + 1, 1, stride=1, stride_axis=0)
        odd = pltpu.roll(v, 1, 1, stride=1, stride_axis=0)
        placed.append((windowed(even), windowed(odd)))
    for c, i in enumerate(NA_CASE_ROWS):
        first = _na_window_start(i) - i + NA_ROWS - 1
        for t in range(NA_ROWS // 2):
            o_ref[c, :, t * LANES:(t + 1) * LANES] = jnp.where(low, placed[first + 2 * t][0],
                                                               placed[first + 2 * t + 1][1])


def _na_bias_table(rel_bias):
    nl = rel_bias.shape[0]
    n_dr = 2 * NA_ROWS - 1
    nc = len(NA_CASE_ROWS)
    pad_lo = (GRID_W - 1) - (NA_COLS - 1)
    rows = jnp.pad(rel_bias.astype(F32), ((0, 0), (0, 0), (0, 0), (pad_lo, LANES - (2 * NA_COLS - 1) - pad_lo)))
    return pl.pallas_call(
        _bias_table_kernel,
        grid=(nl, NA_HEADS),
        in_specs=[pl.BlockSpec((None, None, n_dr, LANES), lambda l, h: (l, h, 0, 0))],
        out_specs=pl.BlockSpec((None, None, nc, GRID_W, NA_KEYS), lambda l, h: (l, h // 2, 0, h % 2, 0)),
        out_shape=jax.ShapeDtypeStruct((nl, NA_HEADS // 2, nc, 2 * GRID_W, NA_KEYS), F32),
        compiler_params=pltpu.CompilerParams(dimension_semantics=("parallel", "parallel")),
        name="bias_table",
    )(rows)


def _out_ffn_kernel(x1_ref, ya_ref, yb_ref, gm_ref, wgate_ref, wa_ref, wb_ref, wout_ref,
                    g3_ref, wup_ref, wdown_ref, gf_ref, o_ref, *, final):
    n_split = TM // SUB_ROWS
    parts = [slice(i * SUB_ROWS, (i + 1) * SUB_ROWS) for i in range(n_split)]
    x1s, merged = [], []
    for r in parts:
        x1 = x1_ref[r, :]
        h = _rms(x1, gm_ref[...]).astype(BF16)
        gates = jax.nn.sigmoid(jnp.dot(h, wgate_ref[...], preferred_element_type=F32))
        ya = jnp.concatenate([ya_ref[c, r, :] for c in range(ya_ref.shape[0])], axis=1)
        yb = jnp.concatenate([yb_ref[c, r, :] for c in range(yb_ref.shape[0])], axis=1)
        ba = jnp.dot(ya, wa_ref[...], preferred_element_type=F32)
        bb = jnp.dot(yb, wb_ref[...], preferred_element_type=F32)
        x1s.append(x1)
        merged.append((gates[:, :D_MODEL] * ba + gates[:, D_MODEL:] * bb).astype(BF16))
    x2s = [x1 + jnp.dot(m, wout_ref[...], preferred_element_type=F32) for x1, m in zip(x1s, merged)]
    h3s = [_rms(x2, g3_ref[...]).astype(BF16) for x2 in x2s]
    for r, x2, h3 in zip(parts, x2s, h3s):
        x3 = x2 + 0.5 * _swiglu_half(h3, wup_ref, wdown_ref)
        if final:
            x3 = _rms(x3, gf_ref[...])
        o_ref[r, :] = x3


def _out_ffn(layer, x1, ya, yb, gm, wgate, wa, wb, wout, g3, wup, wdown, gf, final):
    b = x1.shape[0]
    row = pl.BlockSpec((None, TM, D_MODEL), lambda i, j: (i, j, 0))
    slabs = lambda n: pl.BlockSpec((None, n, TM, LANES), lambda i, j: (i, 0, j, 0))
    weights = (gm, wgate, wa, wb, wout, g3, wup, wdown)
    gf_spec = pl.BlockSpec(gf.shape, lambda i, j: (0, 0))
    return pl.pallas_call(
        functools.partial(_out_ffn_kernel, final=final),
        grid=(b, SEQ // TM),
        in_specs=[row, slabs(ya.shape[1]), slabs(yb.shape[1])] + [_resident(w.shape, layer) for w in weights] + [gf_spec],
        out_specs=row,
        out_shape=jax.ShapeDtypeStruct((b, SEQ, D_MODEL), F32),
        compiler_params=pltpu.CompilerParams(dimension_semantics=("parallel", "parallel"),
                                             vmem_limit_bytes=VMEM_LIMIT_BYTES),
        name="out_ffn",
    )(x1, ya, yb, *weights, gf)


def _rope_tables():
    half = HEAD_DIM // 2
    pos = jnp.arange(SEQ)
    inv_freq = ROPE_THETA ** (-jnp.arange(half, dtype=F32) / half)
    ang = pos.astype(F32)[:, None] * inv_freq[None, :]
    cos = jnp.cos(ang)
    sin = jnp.sin(ang)
    cos_slab = jnp.tile(cos, (1, LANES // half))
    sin_slab = jnp.tile(jnp.concatenate([-sin, sin], axis=-1), (1, LANES // HEAD_DIM))

    def class_major(t, d):
        w = t.shape[-1]
        return t.reshape(SEQ // TM, TM // d, d, w).transpose(0, 2, 1, 3).reshape(SEQ, w)

    slabs = [cos_slab, sin_slab]
    for d in DILATIONS[1:]:
        slabs += [class_major(cos_slab, d), class_major(sin_slab, d)]
    d1 = DILATIONS[1]
    rows = [cos.T, sin.T, class_major(cos, d1).T, class_major(sin, d1).T]
    return jnp.concatenate(slabs, axis=1), jnp.concatenate(rows, axis=0)


def kernel(x, ffn1_norm, ffn1_w_up, ffn1_w_down, mix_norm, w_in, na_rel_bias, w_branch_a, w_branch_b,
           w_out, ffn2_norm, ffn2_w_up, ffn2_w_down, final_norm):
    assert x.shape[1:] == (SEQ, D_MODEL) and x.dtype == F32
    tab, tab_t = _rope_tables()
    gain = lambda g: g.astype(F32).reshape(-1, 1, D_MODEL)
    w = w_in.astype(BF16)
    aq, ak, av = (w[:, :, i * 3 * GROUP_W:(i + 1) * 3 * GROUP_W] for i in range(3))
    bq, bk, bv = (w[:, :, DIL_QKV + i * NA_W:DIL_QKV + (i + 1) * NA_W] for i in range(3))
    grp = lambda t, g: t[:, :, g * GROUP_W:(g + 1) * GROUP_W]
    wnat = jnp.concatenate([grp(aq, 0), grp(av, 0), bq, bv], axis=2)
    wknat = jnp.concatenate([grp(ak, 0), bk], axis=2).transpose(0, 2, 1)
    wg1 = jnp.concatenate([grp(aq, 1), grp(av, 1)], axis=2)
    wkg1 = grp(ak, 1).transpose(0, 2, 1)
    wg2 = jnp.concatenate([grp(aq, 2), grp(ak, 2), grp(av, 2)], axis=2)
    wgate = w[:, :, DIL_QKV + NA_QKV:]
    g_ffn1, g_mix, g_ffn2 = gain(ffn1_norm), gain(mix_norm), gain(ffn2_norm)
    up1, down1 = ffn1_w_up.astype(BF16), ffn1_w_down.astype(BF16)
    up2, down2 = ffn2_w_up.astype(BF16), ffn2_w_down.astype(BF16)
    wa, wb, wo = w_branch_a.astype(BF16), w_branch_b.astype(BF16), w_out.astype(BF16)
    bias_tab = _na_bias_table(na_rel_bias)
    g_final = final_norm.astype(F32).reshape(1, D_MODEL)

    for layer in range(DEPTH):
        x1, a0, kt0, a1, kt1, a2, nb, ktb = _ffn_proj(
            layer, x, g_ffn1, up1, down1, g_mix, wnat, wknat, wg1, wkg1, wg2, tab, tab_t)
        ya = _dilated(a0, kt0, a1, kt1, a2)
        yb = _nbr(layer, nb, ktb, bias_tab)
        x = _out_ffn(layer, x1, ya, yb, g_mix, wgate, wa, wb, wo, g_ffn2, up2, down2, g_final,
                     final=(layer == DEPTH - 1))
    return x
```

```python
import functools

import numpy as np
import jax
import jax.numpy as jnp
from jax import lax
from jax.experimental import pallas as pl
from jax.experimental.pallas import tpu as pltpu

F32 = jnp.float32
BF16 = jnp.bfloat16

D_MODEL = 1024
SEQ = 2048
DEPTH = 2
HEAD_DIM = 64
DILATIONS = (1, 4, 16)
BAND = 64
GROUP_HEADS = 4
GROUP_W = GROUP_HEADS * HEAD_DIM
NA_HEADS = 8
NA_W = NA_HEADS * HEAD_DIM
GRID_W = 64
GRID_ROWS = SEQ // GRID_W
NA_ROWS = 8
NA_COLS = 16
D_FF = 2816
ROPE_THETA = 10000.0
RMS_EPS = 1e-6
NEG_INF = -1e30
DIL_QKV = 3 * 3 * GROUP_W
NA_QKV = 3 * NA_W

LANES = 128
VMEM_LIMIT_BYTES = 56 * 1024 * 1024

TM = 512
MXU_TILE = 256
SUB_ROWS = 256
FF_CHUNKS = (6 * MXU_TILE, 5 * MXU_TILE)
assert sum(FF_CHUNKS) == D_FF
NA_KEYS = NA_ROWS * GRID_W
DIL_PIPE_DEPTH = 4
NA_PIPE_DEPTH = 3
NA_PAIRS_PER_STEP = 4
NA_CASE_ROWS = (0, 1, 2, 3, 4, 29, 30, 31)


def _rms(x, g):
    return x * lax.rsqrt(jnp.mean(x * x, axis=-1, keepdims=True) + RMS_EPS) * g


def _swiglu_half(h, w_up_ref, w_down_ref):
    acc = None
    lo = 0
    for width in FF_CHUNKS:
        g = jnp.dot(h, w_up_ref[:, lo:lo + width], preferred_element_type=F32)
        u = jnp.dot(h, w_up_ref[:, D_FF + lo:D_FF + lo + width], preferred_element_type=F32)
        a = (g * jax.nn.sigmoid(g) * u).astype(BF16)
        part = jnp.dot(a, w_down_ref[lo:lo + width, :], preferred_element_type=F32)
        acc = part if acc is None else acc + part
        lo += width
    return acc


def _rope(t, cos, sin_signed):
    lane = lax.broadcasted_iota(jnp.int32, t.shape, 1)
    first_half = (lane % HEAD_DIM) < (HEAD_DIM // 2)
    partner = jnp.where(first_half, pltpu.roll(t, LANES - HEAD_DIM // 2, 1), pltpu.roll(t, HEAD_DIM // 2, 1))
    return t * cos + partner * sin_signed


def _ffn_proj_kernel(x_ref, g1_ref, wup_ref, wdown_ref, g2_ref, wnat_ref, wknat_ref, wg1_ref, wkg1_ref, wg2_ref,
                     tab_ref, tabt_ref,
                     x1_ref, a0_ref, kt0_ref, a1_ref, kt1_ref, a2_ref, nb_ref, ktb_ref,
                     slab_ref, hperm_ref):
    parts = [slice(i * SUB_ROWS, (i + 1) * SUB_ROWS) for i in range(TM // SUB_ROWS)]
    n_slabs = D_MODEL // LANES
    h2bs = []
    for r in parts:
        x = x_ref[r, :]
        h = _rms(x, g1_ref[...]).astype(BF16)
        x1 = x + 0.5 * _swiglu_half(h, wup_ref, wdown_ref)
        x1_ref[r, :] = x1
        h2 = _rms(x1, g2_ref[...])
        for c in range(n_slabs):
            slab_ref[c, r, :] = h2[:, c * LANES:(c + 1) * LANES]
        h2bs.append(h2.astype(BF16))

    scale = HEAD_DIM ** -0.5
    half = HEAD_DIM // 2
    nt_dims = (((1,), (1,)), ((), ()))

    def tables(order, r):
        cos = tab_ref[r, (2 * order) * LANES:(2 * order + 1) * LANES]
        sin = tab_ref[r, (2 * order + 1) * LANES:(2 * order + 2) * LANES]
        return cos, sin

    def put_slabs(a_ref, first, t, r=slice(None)):
        for c in range(t.shape[1] // LANES):
            a_ref[first + c, r, :] = t[:, c * LANES:(c + 1) * LANES].astype(BF16)

    def head_masked(t):
        lane = lax.broadcasted_iota(jnp.int32, t.shape, 1)
        first = (lane % LANES) < HEAD_DIM
        return jnp.where(first, t, 0.0), jnp.where(first, 0.0, t)

    def rope_cols(p, lo, order, mul, r=slice(None)):
        cos, sin = tables(order, r)
        return jnp.concatenate([_rope(p[:, lo + c * LANES:lo + (c + 1) * LANES] * mul, cos, sin)
                                for c in range(GROUP_W // LANES)], axis=1)

    def write_kt(kt_ref, kt, order, r=slice(None)):
        cos = tabt_ref[(2 * order) * half:(2 * order + 1) * half, r]
        sin = tabt_ref[(2 * order + 1) * half:(2 * order + 2) * half, r]
        for hh in range(GROUP_HEADS):
            t1 = kt[hh * HEAD_DIM:hh * HEAD_DIM + half, :]
            t2 = kt[hh * HEAD_DIM + half:(hh + 1) * HEAD_DIM, :]
            kt_ref[hh * HEAD_DIM:hh * HEAD_DIM + half, r] = (t1 * cos - t2 * sin).astype(BF16)
            kt_ref[hh * HEAD_DIM + half:(hh + 1) * HEAD_DIM, r] = (t2 * cos + t1 * sin).astype(BF16)

    n_pairs = GROUP_W // LANES
    nb_pairs = NA_W // LANES

    for r, h2b in zip(parts, h2bs):
        pn = jnp.dot(h2b, wnat_ref[...], preferred_element_type=F32)
        put_slabs(a0_ref, 0, rope_cols(pn, 0, 0, scale, r), r)
        put_slabs(a0_ref, n_pairs, pn[:, GROUP_W:2 * GROUP_W], r)
        qb_a, qb_b = head_masked(pn[:, 2 * GROUP_W:2 * GROUP_W + NA_W] * scale)
        put_slabs(nb_ref, 0, qb_a, r)
        put_slabs(nb_ref, nb_pairs, qb_b, r)
        put_slabs(nb_ref, 2 * nb_pairs, pn[:, 2 * GROUP_W + NA_W:], r)
        ktn = lax.dot_general(wknat_ref[...], h2b, nt_dims, preferred_element_type=F32)
        write_kt(kt0_ref, ktn[:GROUP_W], 0, r)
        ktb_ref[:, r] = ktn[GROUP_W:].astype(BF16)

    def permute(d):
        n = TM // d
        for r in range(d):
            for c in range(n_slabs):
                hperm_ref[r * n:(r + 1) * n, c * LANES:(c + 1) * LANES] = (
                    slab_ref[c, pl.ds(r, n, stride=d), :].astype(BF16))

    permute(DILATIONS[1])
    pg = jnp.dot(hperm_ref[...], wg1_ref[...], preferred_element_type=F32)
    put_slabs(a1_ref, 0, rope_cols(pg, 0, 1, scale))
    put_slabs(a1_ref, n_pairs, pg[:, GROUP_W:])
    kt1 = lax.dot_general(wkg1_ref[...], hperm_ref[...], nt_dims, preferred_element_type=F32)
    write_kt(kt1_ref, kt1, 1)

    permute(DILATIONS[2])
    pg = jnp.dot(hperm_ref[...], wg2_ref[...], preferred_element_type=F32)
    put_slabs(a2_ref, 0, rope_cols(pg, 0, 2, scale))
    k2_a, k2_b = head_masked(rope_cols(pg, GROUP_W, 2, 1.0))
    put_slabs(a2_ref, n_pairs, k2_a)
    put_slabs(a2_ref, 2 * n_pairs, k2_b)
    put_slabs(a2_ref, 3 * n_pairs, pg[:, 2 * GROUP_W:])


def _resident(stacked_shape, layer):
    nd = len(stacked_shape) - 1
    return pl.BlockSpec((None,) + tuple(stacked_shape[1:]), lambda *_: (layer,) + (0,) * nd,
                        pipeline_mode=pl.Buffered(1))


def _ffn_proj(layer, x, g1, wup, wdown, g2, wnat, wknat, wg1, wkg1, wg2, tab, tab_t):
    b = x.shape[0]
    nt = SEQ // TM
    n_pairs = GROUP_W // LANES
    nb_pairs = NA_W // LANES
    row = pl.BlockSpec((None, TM, D_MODEL), lambda i, j: (i, j, 0))
    slabs = lambda n: pl.BlockSpec((None, n, TM, LANES), lambda i, j: (i, 0, j, 0))
    col = lambda h: pl.BlockSpec((None, h, TM), lambda i, j: (i, 0, j))
    slab_shape = lambda n: jax.ShapeDtypeStruct((b, n, SEQ, LANES), BF16)
    out_shape = (
        jax.ShapeDtypeStruct((b, SEQ, D_MODEL), F32),
        slab_shape(2 * n_pairs),
        jax.ShapeDtypeStruct((b, GROUP_W, SEQ), BF16),
        slab_shape(2 * n_pairs),
        jax.ShapeDtypeStruct((b, GROUP_W, SEQ), BF16),
        slab_shape(4 * n_pairs),
        slab_shape(3 * nb_pairs),
        jax.ShapeDtypeStruct((b, NA_W, SEQ), BF16),
    )
    weights = (g1, wup, wdown, g2, wnat, wknat, wg1, wkg1, wg2)
    return pl.pallas_call(
        _ffn_proj_kernel,
        grid=(b, nt),
        in_specs=[row] + [_resident(w.shape, layer) for w in weights]
        + [pl.BlockSpec((TM, tab.shape[1]), lambda i, j: (j, 0)), pl.BlockSpec((tab_t.shape[0], TM), lambda i, j: (0, j))],
        out_specs=(row, slabs(2 * n_pairs), col(GROUP_W), slabs(2 * n_pairs), col(GROUP_W),
                   slabs(4 * n_pairs), slabs(3 * nb_pairs), col(NA_W)),
        out_shape=out_shape,
        scratch_shapes=[pltpu.VMEM((D_MODEL // LANES, TM, LANES), F32), pltpu.VMEM((TM, D_MODEL), BF16)],
        compiler_params=pltpu.CompilerParams(dimension_semantics=("parallel", "parallel"),
                                             vmem_limit_bytes=VMEM_LIMIT_BYTES),
        name="ffn_proj",
    )(x, *weights, tab, tab_t)


def _stacked_scores(q_stack, keys, bias):
    return [jnp.dot(q_stack, keys, preferred_element_type=F32) + bias]


def _split_scores(q, keys, bias):
    out = []
    for h in range(2):
        if isinstance(keys, (tuple, list)):
            s = lax.dot_general(q, keys[h], (((1,), (1,)), ((), ())), preferred_element_type=F32)
        else:
            zeros = jnp.zeros((HEAD_DIM, keys.shape[1]), BF16)
            own = keys[h * HEAD_DIM:(h + 1) * HEAD_DIM]
            s = jnp.dot(q, jnp.concatenate([own, zeros] if h == 0 else [zeros, own], axis=0),
                        preferred_element_type=F32)
        out.append(s + bias)
    return out


def _pair_finish(scores, v):
    vx = jnp.concatenate([v, jnp.ones(v.shape, BF16)], axis=1)
    res = []
    for sc in scores:
        m = jnp.max(sc, axis=-1, keepdims=True)
        p = jnp.exp(sc - m).astype(BF16)
        res.append((jnp.dot(p, vx, preferred_element_type=F32), m))
    if len(res) == 1:
        half = res[0][0].shape[0] // 2
        res = [(res[0][0][h * half:(h + 1) * half], res[0][1][h * half:(h + 1) * half]) for h in range(2)]
    shape = (res[0][0].shape[0], LANES)
    left = lax.broadcasted_iota(jnp.int32, shape, 1) < HEAD_DIM
    acc = jnp.where(left, res[0][0][:, :LANES], res[1][0][:, :LANES])
    l = jnp.where(left, res[0][0][:, LANES:], res[1][0][:, LANES:])
    m = jnp.where(left, jnp.broadcast_to(res[0][1], shape), jnp.broadcast_to(res[1][1], shape))
    return acc, m, l


def _software_pipeline(blocks, depth):
    pending = []
    for score_fn, finish_fn in blocks:
        pending.append((finish_fn, score_fn()))
        if len(pending) > depth:
            fn, sc = pending.pop(0)
            fn(sc)
    for fn, sc in pending:
        fn(sc)


CHUNK = 2 * BAND


def _band_bias(m, n, lo, hi):
    u = lax.broadcasted_iota(jnp.int32, (m, n), 0)
    w = lax.broadcasted_iota(jnp.int32, (m, n), 1)
    d = w - u
    return jnp.where(d >= lo, jnp.where(d <= hi, 0.0, NEG_INF), NEG_INF).astype(F32)


def _dilated_kernel(a0_ref, kt0_full, a1_ref, kt1_full, a2_ref, o_full, out_full, lse_full):
    n_pairs = GROUP_W // LANES
    blocks = []
    for p in range(n_pairs):
        _dilated_pair(blocks, a0_ref.at[p], a0_ref.at[n_pairs + p], kt0_full.at[p * LANES:(p + 1) * LANES],
                      a1_ref.at[p], a1_ref.at[n_pairs + p], kt1_full.at[p * LANES:(p + 1) * LANES],
                      a2_ref.at[p], a2_ref.at[n_pairs + p], a2_ref.at[2 * n_pairs + p], a2_ref.at[3 * n_pairs + p],
                      o_full.at[p], out_full.at[p], lse_full.at[p])
    _software_pipeline(blocks, DIL_PIPE_DEPTH)


def _dilated_pair(blocks, q0_ref, v0_ref, kt0_ref, q1_ref, v1_ref, kt1_ref, q2_ref, k2a_ref, k2b_ref, v2_ref,
                  o_ref, out_s, lse_s):
    q_refs = (q0_ref, q1_ref, q2_ref)
    v_refs = (v0_ref, v1_ref, v2_ref)

    bias_first = _band_bias(BAND, CHUNK, -BAND, BAND)
    bias_last = _band_bias(BAND, CHUNK, 0, CHUNK)
    bias_full = _band_bias(CHUNK, 2 * CHUNK, 0, CHUNK)
    bias_one = _band_bias(CHUNK, CHUNK, -BAND, BAND)

    def normalise(vals):
        acc, m, l = vals
        return acc / l, m + jnp.log(l)

    def stash(g, outs, vals):
        o, lse = normalise(vals)
        for row0, n, dst, stride in outs:
            idx = pl.ds(dst, n, stride=stride)
            out_s[g - 1, idx, :] = o[row0:row0 + n]
            lse_s[g - 1, idx, :] = lse[row0:row0 + n]

    def merge(g, outs, vals):
        assert g == 0
        o0, lse0 = normalise(vals)
        for row0, n, dst, stride in outs:
            assert stride == 1
            sl = slice(dst, dst + n)
            os_ = [o0[row0:row0 + n], out_s[0, sl, :], out_s[1, sl, :]]
            ls_ = [lse0[row0:row0 + n], lse_s[0, sl, :], lse_s[1, sl, :]]
            mx = jnp.maximum(jnp.maximum(ls_[0], ls_[1]), ls_[2])
            ws = [jnp.exp(x - mx) for x in ls_]
            num = ws[0] * os_[0] + ws[1] * os_[1] + ws[2] * os_[2]
            o_ref[sl, :] = (num / (ws[0] + ws[1] + ws[2])).astype(BF16)

    def add(g, q_slices, keys_fn, v_slices, bias, outs):
        def score_fn():
            q = jnp.concatenate([q_refs[g][sl, :] for sl in q_slices], axis=0)
            return _split_scores(q, keys_fn(), bias)

        def finish_fn(scores):
            v = jnp.concatenate([v_refs[g][sl, :] for sl in v_slices], axis=0)
            (merge if g == 0 else stash)(g, outs, _pair_finish(scores, v))

        blocks.append((score_fn, finish_fn))

    def class_blocks(g, kt_ref, base, nat, stride):
        nc = len(base)
        rows = lambda j, lo, n: slice(base[j] + lo, base[j] + lo + n)
        add(g, [rows(0, 0, BAND)], lambda: kt_ref[:, rows(0, 0, CHUNK)], [rows(0, 0, CHUNK)],
            bias_first, [(0, BAND, nat[0], stride)])
        for j in range(nc - 1):
            add(g, [rows(j, BAND, BAND), rows(j + 1, 0, BAND)],
                lambda j=j: jnp.concatenate([kt_ref[:, rows(j, 0, CHUNK)], kt_ref[:, rows(j + 1, 0, CHUNK)]], axis=1),
                [rows(j, 0, CHUNK), rows(j + 1, 0, CHUNK)],
                bias_full, [(0, BAND, nat[j] + BAND * stride, stride), (BAND, BAND, nat[j + 1], stride)])
        add(g, [rows(nc - 1, BAND, BAND)], lambda: kt_ref[:, rows(nc - 1, 0, CHUNK)],
            [rows(nc - 1, 0, CHUNK)], bias_last, [(0, BAND, nat[nc - 1] + BAND * stride, stride)])

    nt = SEQ // TM
    d2 = DILATIONS[2]
    piece = TM // d2
    for r in range(d2):
        pieces = [slice(j * TM + r * piece, j * TM + (r + 1) * piece) for j in range(nt)]
        add(2, pieces,
            lambda pieces=pieces: [jnp.concatenate([k_ref[sl, :] for sl in pieces], axis=0) for k_ref in (k2a_ref, k2b_ref)],
            pieces, bias_one, [(j * piece, piece, j * TM + r, d2) for j in range(nt)])

    d1 = DILATIONS[1]
    for r in range(d1):
        class_blocks(1, kt1_ref, [j * TM + r * CHUNK for j in range(nt)], [j * TM + r for j in range(nt)], d1)

    chunk_rows = [c * CHUNK for c in range(SEQ // CHUNK)]
    class_blocks(0, kt0_ref, chunk_rows, chunk_rows, 1)


def _dilated(a0, kt0, a1, kt1, a2):
    b = a0.shape[0]
    n_pairs = GROUP_W // LANES
    slabs = lambda n: pl.BlockSpec((None, n, SEQ, LANES), lambda i: (i, 0, 0, 0))
    ktspec = pl.BlockSpec((None, GROUP_W, SEQ), lambda i: (i, 0, 0))
    return pl.pallas_call(
        _dilated_kernel,
        grid=(b,),
        in_specs=[slabs(2 * n_pairs), ktspec, slabs(2 * n_pairs), ktspec, slabs(4 * n_pairs)],
        out_specs=slabs(n_pairs),
        out_shape=jax.ShapeDtypeStruct((b, n_pairs, SEQ, LANES), BF16),
        scratch_shapes=[pltpu.VMEM((n_pairs, 2, SEQ, LANES), F32), pltpu.VMEM((n_pairs, 2, SEQ, LANES), F32)],
        compiler_params=pltpu.CompilerParams(dimension_semantics=("parallel",),
                                             vmem_limit_bytes=VMEM_LIMIT_BYTES),
        name="dilated",
    )(a0, kt0, a1, kt1, a2)


def _na_window_start(i):
    return int(np.clip(i - NA_ROWS // 2, 0, GRID_ROWS - NA_ROWS))


def _na_kernel(nb_ref, kt_full, bias_full, o_full, kts_full):
    blocks = []
    for p in range(NA_PAIRS_PER_STEP):
        _na_pair(blocks, nb_ref.at[0, p], nb_ref.at[1, p], nb_ref.at[2, p],
                 kt_full.at[p * LANES:(p + 1) * LANES], bias_full.at[p], o_full.at[p], kts_full.at[p])
    _software_pipeline(blocks, NA_PIPE_DEPTH)


def _na_pair(blocks, qa_ref, qb_ref, v_ref, kt_ref, bias_ref, o_ref, kts_ref):
    words = pltpu.bitcast(kt_ref[...], jnp.uint32)
    kts_ref[...] = pltpu.bitcast(pltpu.roll(words, SEQ - GRID_W, 1), BF16)

    for i in range(GRID_ROWS):
        lo = _na_window_start(i)
        case = NA_CASE_ROWS.index(i) if i in NA_CASE_ROWS else NA_CASE_ROWS.index(NA_ROWS // 2)
        qsl = slice(i * GRID_W, (i + 1) * GRID_W)
        vsl = slice(lo * GRID_W, lo * GRID_W + NA_KEYS)
        if lo % 2 == 0:
            keys_fn = lambda lo=lo: kt_ref[:, lo * GRID_W:lo * GRID_W + NA_KEYS]
        else:
            keys_fn = lambda lo=lo: kts_ref[:, (lo - 1) * GRID_W:(lo - 1) * GRID_W + NA_KEYS]

        def score_fn(qsl=qsl, keys_fn=keys_fn, case=case):
            q_stack = jnp.concatenate([qa_ref[qsl, :], qb_ref[qsl, :]], axis=0)
            return _stacked_scores(q_stack, keys_fn(), bias_ref[case])

        def finish_fn(scores, qsl=qsl, vsl=vsl):
            acc, _, l = _pair_finish(scores, v_ref[vsl, :])
            o_ref[qsl, :] = (acc / l).astype(BF16)

        blocks.append((score_fn, finish_fn))


def _nbr(layer, nb, ktb, bias_tab):
    b = nb.shape[0]
    n_pairs = NA_W // LANES
    pps = NA_PAIRS_PER_STEP
    nb_view = nb.reshape(b, 3, n_pairs // pps, pps, SEQ, LANES)
    return pl.pallas_call(
        _na_kernel,
        grid=(n_pairs // pps, b),
        in_specs=[pl.BlockSpec((None, 3, None, pps, SEQ, LANES), lambda g, i: (i, 0, g, 0, 0, 0)),
                  pl.BlockSpec((None, pps * LANES, SEQ), lambda g, i: (i, g, 0)),
                  pl.BlockSpec((None, pps, len(NA_CASE_ROWS), 2 * GRID_W, NA_KEYS), lambda g, i: (layer, g, 0, 0, 0))],
        out_specs=pl.BlockSpec((None, pps, SEQ, LANES), lambda g, i: (i, g, 0, 0)),
        out_shape=jax.ShapeDtypeStruct((b, n_pairs, SEQ, LANES), BF16),
        scratch_shapes=[pltpu.VMEM((pps, LANES, SEQ), BF16)],
        compiler_params=pltpu.CompilerParams(dimension_semantics=("parallel", "parallel"),
                                             vmem_limit_bytes=VMEM_LIMIT_BYTES),
        name="nbr",
    )(nb_view, ktb, bias_tab)


def _bias_table_kernel(rb_ref, o_ref):
    n = GRID_W
    qc = lax.broadcasted_iota(jnp.int32, (n, LANES), 0)
    lane = lax.broadcasted_iota(jnp.int32, (n, LANES), 1)
    kc = lane % n
    win_lo = jnp.clip(qc - NA_COLS // 2, 0, n - NA_COLS)
    windowed = lambda t: jnp.where(kc >= win_lo, jnp.where(kc < win_lo + NA_COLS, t, NEG_INF), NEG_INF)
    low = lane < n
    placed = []
    for i in range(2 * NA_ROWS - 1):
        v = jnp.broadcast_to(rb_ref[i:i + 1, :], (n, LANES))
        even = pltpu.roll(v, n + 1, 1, stride=1, stride_axis=0)
        odd = pltpu.roll(v, 1, 1, stride=1, stride_axis=0)
        placed.append((windowed(even), windowed(odd)))
    for c, i in enumerate(NA_CASE_ROWS):
        first = _na_window_start(i) - i + NA_ROWS - 1
        for t in range(NA_ROWS // 2):
            o_ref[c, :, t * LANES:(t + 1) * LANES] = jnp.where(low, placed[first + 2 * t][0],
                                                               placed[first + 2 * t + 1][1])


def _na_bias_table(rel_bias):
    nl = rel_bias.shape[0]
    n_dr = 2 * NA_ROWS - 1
    nc = len(NA_CASE_ROWS)
    pad_lo = (GRID_W - 1) - (NA_COLS - 1)
    rows = jnp.pad(rel_bias.astype(F32), ((0, 0), (0, 0), (0, 0), (pad_lo, LANES - (2 * NA_COLS - 1) - pad_lo)))
    return pl.pallas_call(
        _bias_table_kernel,
        grid=(nl, NA_HEADS),
        in_specs=[pl.BlockSpec((None, None, n_dr, LANES), lambda l, h: (l, h, 0, 0))],
        out_specs=pl.BlockSpec((None, None, nc, GRID_W, NA_KEYS), lambda l, h: (l, h // 2, 0, h % 2, 0)),
        out_shape=jax.ShapeDtypeStruct((nl, NA_HEADS // 2, nc, 2 * GRID_W, NA_KEYS), F32),
        compiler_params=pltpu.CompilerParams(dimension_semantics=("parallel", "parallel")),
        name="bias_table",
    )(rows)


def _out_ffn_kernel(x1_ref, ya_ref, yb_ref, gm_ref, wgate_ref, wa_ref, wb_ref, wout_ref,
                    g3_ref, wup_ref, wdown_ref, gf_ref, o_ref, *, final):
    n_split = TM // SUB_ROWS
    parts = [slice(i * SUB_ROWS, (i + 1) * SUB_ROWS) for i in range(n_split)]
    x1s, merged = [], []
    for r in parts:
        x1 = x1_ref[r, :]
        h = _rms(x1, gm_ref[...]).astype(BF16)
        gates = jax.nn.sigmoid(jnp.dot(h, wgate_ref[...], preferred_element_type=F32))
        ya = jnp.concatenate([ya_ref[c, r, :] for c in range(ya_ref.shape[0])], axis=1)
        yb = jnp.concatenate([yb_ref[c, r, :] for c in range(yb_ref.shape[0])], axis=1)
        ba = jnp.dot(ya, wa_ref[...], preferred_element_type=F32)
        bb = jnp.dot(yb, wb_ref[...], preferred_element_type=F32)
        x1s.append(x1)
        merged.append((gates[:, :D_MODEL] * ba + gates[:, D_MODEL:] * bb).astype(BF16))
    x2s = [x1 + jnp.dot(m, wout_ref[...], preferred_element_type=F32) for x1, m in zip(x1s, merged)]
    h3s = [_rms(x2, g3_ref[...]).astype(BF16) for x2 in x2s]
    for r, x2, h3 in zip(parts, x2s, h3s):
        x3 = x2 + 0.5 * _swiglu_half(h3, wup_ref, wdown_ref)
        if final:
            x3 = _rms(x3, gf_ref[...])
        o_ref[r, :] = x3


def _out_ffn(layer, x1, ya, yb, gm, wgate, wa, wb, wout, g3, wup, wdown, gf, final):
    b = x1.shape[0]
    row = pl.BlockSpec((None, TM, D_MODEL), lambda i, j: (i, j, 0))
    slabs = lambda n: pl.BlockSpec((None, n, TM, LANES), lambda i, j: (i, 0, j, 0))
    weights = (gm, wgate, wa, wb, wout, g3, wup, wdown)
    gf_spec = pl.BlockSpec(gf.shape, lambda i, j: (0, 0))
    return pl.pallas_call(
        functools.partial(_out_ffn_kernel, final=final),
        grid=(b, SEQ // TM),
        in_specs=[row, slabs(ya.shape[1]), slabs(yb.shape[1])] + [_resident(w.shape, layer) for w in weights] + [gf_spec],
        out_specs=row,
        out_shape=jax.ShapeDtypeStruct((b, SEQ, D_MODEL), F32),
        compiler_params=pltpu.CompilerParams(dimension_semantics=("parallel", "parallel"),
                                             vmem_limit_bytes=VMEM_LIMIT_BYTES),
        name="out_ffn",
    )(x1, ya, yb, *weights, gf)


def _rope_tables():
    half = HEAD_DIM // 2
    pos = jnp.arange(SEQ)
    inv_freq = ROPE_THETA ** (-jnp.arange(half, dtype=F32) / half)
    ang = pos.astype(F32)[:, None] * inv_freq[None, :]
    cos = jnp.cos(ang)
    sin = jnp.sin(ang)
    cos_slab = jnp.tile(cos, (1, LANES // half))
    sin_slab = jnp.tile(jnp.concatenate([-sin, sin], axis=-1), (1, LANES // HEAD_DIM))

    def class_major(t, d):
        w = t.shape[-1]
        return t.reshape(SEQ // TM, TM // d, d, w).transpose(0, 2, 1, 3).reshape(SEQ, w)

    slabs = [cos_slab, sin_slab]
    for d in DILATIONS[1:]:
        slabs += [class_major(cos_slab, d), class_major(sin_slab, d)]
    d1 = DILATIONS[1]
    rows = [cos.T, sin.T, class_major(cos, d1).T, class_major(sin, d1).T]
    return jnp.concatenate(slabs, axis=1), jnp.concatenate(rows, axis=0)


def kernel(x, ffn1_norm, ffn1_w_up, ffn1_w_down, mix_norm, w_in, na_rel_bias, w_branch_a, w_branch_b,
           w_out, ffn2_norm, ffn2_w_up, ffn2_w_down, final_norm):
    assert x.shape[1:] == (SEQ, D_MODEL) and x.dtype == F32
    tab, tab_t = _rope_tables()
    gain = lambda g: g.astype(F32).reshape(-1, 1, D_MODEL)
    w = w_in.astype(BF16)
    aq, ak, av = (w[:, :, i * 3 * GROUP_W:(i + 1) * 3 * GROUP_W] for i in range(3))
    bq, bk, bv = (w[:, :, DIL_QKV + i * NA_W:DIL_QKV + (i + 1) * NA_W] for i in range(3))
    grp = lambda t, g: t[:, :, g * GROUP_W:(g + 1) * GROUP_W]
    wnat = jnp.concatenate([grp(aq, 0), grp(av, 0), bq, bv], axis=2)
    wknat = jnp.concatenate([grp(ak, 0), bk], axis=2).transpose(0, 2, 1)
    wg1 = jnp.concatenate([grp(aq, 1), grp(av, 1)], axis=2)
    wkg1 = grp(ak, 1).transpose(0, 2, 1)
    wg2 = jnp.concatenate([grp(aq, 2), grp(ak, 2), grp(av, 2)], axis=2)
    wgate = w[:, :, DIL_QKV + NA_QKV:]
    g_ffn1, g_mix, g_ffn2 = gain(ffn1_norm), gain(mix_norm), gain(ffn2_norm)
    up1, down1 = ffn1_w_up.astype(BF16), ffn1_w_down.astype(BF16)
    up2, down2 = ffn2_w_up.astype(BF16), ffn2_w_down.astype(BF16)
    wa, wb, wo = w_branch_a.astype(BF16), w_branch_b.astype(BF16), w_out.astype(BF16)
    bias_tab = _na_bias_table(na_rel_bias)
    g_final = final_norm.astype(F32).reshape(1, D_MODEL)

    for layer in range(DEPTH):
        x1, a0, kt0, a1, kt1, a2, nb, ktb = _ffn_proj(
            layer, x, g_ffn1, up1, down1, g_mix, wnat, wknat, wg1, wkg1, wg2, tab, tab_t)
        ya = _dilated(a0, kt0, a1, kt1, a2)
        yb = _nbr(layer, nb, ktb, bias_tab)
        x = _out_ffn(layer, x1, ya, yb, g_mix, wgate, wa, wb, wo, g_ffn2, up2, down2, g_final,
                     final=(layer == DEPTH - 1))
    return x
```

```python
import functools

import numpy as np
import jax
import jax.numpy as jnp
from jax import lax
from jax.experimental import pallas as pl
from jax.experimental.pallas import tpu as pltpu

F32 = jnp.float32
BF16 = jnp.bfloat16

D_MODEL = 1024
SEQ = 2048
DEPTH = 2
HEAD_DIM = 64
DILATIONS = (1, 4, 16)
BAND = 64
GROUP_HEADS = 4
GROUP_W = GROUP_HEADS * HEAD_DIM
NA_HEADS = 8
NA_W = NA_HEADS * HEAD_DIM
GRID_W = 64
GRID_ROWS = SEQ // GRID_W
NA_ROWS = 8
NA_COLS = 16
D_FF = 2816
ROPE_THETA = 10000.0
RMS_EPS = 1e-6
NEG_INF = -1e30
DIL_QKV = 3 * 3 * GROUP_W
NA_QKV = 3 * NA_W

LANES = 128
VMEM_LIMIT_BYTES = 56 * 1024 * 1024

TM = 512
MXU_TILE = 256
TM_OUT = 1024
SUB_ROWS = 256
FF_CHUNKS = (6 * MXU_TILE, 5 * MXU_TILE)
assert sum(FF_CHUNKS) == D_FF
NA_KEYS = NA_ROWS * GRID_W
DIL_PIPE_DEPTH = 4
NA_PIPE_DEPTH = 3
NA_PAIRS_PER_STEP = 4
NA_CASE_ROWS = (0, 1, 2, 3, 4, 29, 30, 31)


def _rms(x, g):
    return x * lax.rsqrt(jnp.mean(x * x, axis=-1, keepdims=True) + RMS_EPS) * g


def _swiglu_half(h, w_up_ref, w_down_ref):
    acc = None
    lo = 0
    for width in FF_CHUNKS:
        g = jnp.dot(h, w_up_ref[:, lo:lo + width], preferred_element_type=F32)
        u = jnp.dot(h, w_up_ref[:, D_FF + lo:D_FF + lo + width], preferred_element_type=F32)
        a = (g * jax.nn.sigmoid(g) * u).astype(BF16)
        part = jnp.dot(a, w_down_ref[lo:lo + width, :], preferred_element_type=F32)
        acc = part if acc is None else acc + part
        lo += width
    return acc


def _rope(t, cos, sin_signed):
    lane = lax.broadcasted_iota(jnp.int32, t.shape, 1)
    first_half = (lane % HEAD_DIM) < (HEAD_DIM // 2)
    partner = jnp.where(first_half, pltpu.roll(t, LANES - HEAD_DIM // 2, 1), pltpu.roll(t, HEAD_DIM // 2, 1))
    return t * cos + partner * sin_signed


def _ffn_proj_kernel(x_ref, g1_ref, wup_ref, wdown_ref, g2_ref, wnat_ref, wknat_ref, wg1_ref, wkg1_ref, wg2_ref,
                     tab_ref, tabt_ref,
                     x1_ref, a0_ref, kt0_ref, a1_ref, kt1_ref, a2_ref, nb_ref, ktb_ref,
                     slab_ref, hperm_ref):
    parts = [slice(i * SUB_ROWS, (i + 1) * SUB_ROWS) for i in range(TM // SUB_ROWS)]
    n_slabs = D_MODEL // LANES
    h2bs = []
    for r in parts:
        x = x_ref[r, :]
        h = _rms(x, g1_ref[...]).astype(BF16)
        x1 = x + 0.5 * _swiglu_half(h, wup_ref, wdown_ref)
        x1_ref[r, :] = x1
        h2 = _rms(x1, g2_ref[...])
        for c in range(n_slabs):
            slab_ref[c, r, :] = h2[:, c * LANES:(c + 1) * LANES]
        h2bs.append(h2.astype(BF16))

    scale = HEAD_DIM ** -0.5
    half = HEAD_DIM // 2
    nt_dims = (((1,), (1,)), ((), ()))

    def tables(order, r):
        cos = tab_ref[r, (2 * order) * LANES:(2 * order + 1) * LANES]
        sin = tab_ref[r, (2 * order + 1) * LANES:(2 * order + 2) * LANES]
        return cos, sin

    def put_slabs(a_ref, first, t, r=slice(None)):
        for c in range(t.shape[1] // LANES):
            a_ref[first + c, r, :] = t[:, c * LANES:(c + 1) * LANES].astype(BF16)

    def head_masked(t):
        lane = lax.broadcasted_iota(jnp.int32, t.shape, 1)
        first = (lane % LANES) < HEAD_DIM
        return jnp.where(first, t, 0.0), jnp.where(first, 0.0, t)

    def rope_cols(p, lo, order, mul, r=slice(None)):
        cos, sin = tables(order, r)
        return jnp.concatenate([_rope(p[:, lo + c * LANES:lo + (c + 1) * LANES] * mul, cos, sin)
                                for c in range(GROUP_W // LANES)], axis=1)

    def write_kt(kt_ref, kt, order, r=slice(None)):
        cos = tabt_ref[(2 * order) * half:(2 * order + 1) * half, r]
        sin = tabt_ref[(2 * order + 1) * half:(2 * order + 2) * half, r]
        for hh in range(GROUP_HEADS):
            t1 = kt[hh * HEAD_DIM:hh * HEAD_DIM + half, :]
            t2 = kt[hh * HEAD_DIM + half:(hh + 1) * HEAD_DIM, :]
            kt_ref[hh * HEAD_DIM:hh * HEAD_DIM + half, r] = (t1 * cos - t2 * sin).astype(BF16)
            kt_ref[hh * HEAD_DIM + half:(hh + 1) * HEAD_DIM, r] = (t2 * cos + t1 * sin).astype(BF16)

    n_pairs = GROUP_W // LANES
    nb_pairs = NA_W // LANES

    for r, h2b in zip(parts, h2bs):
        pn = jnp.dot(h2b, wnat_ref[...], preferred_element_type=F32)
        put_slabs(a0_ref, 0, rope_cols(pn, 0, 0, scale, r), r)
        put_slabs(a0_ref, n_pairs, pn[:, GROUP_W:2 * GROUP_W], r)
        qb_a, qb_b = head_masked(pn[:, 2 * GROUP_W:2 * GROUP_W + NA_W] * scale)
        put_slabs(nb_ref, 0, qb_a, r)
        put_slabs(nb_ref, nb_pairs, qb_b, r)
        put_slabs(nb_ref, 2 * nb_pairs, pn[:, 2 * GROUP_W + NA_W:], r)
        ktn = lax.dot_general(wknat_ref[...], h2b, nt_dims, preferred_element_type=F32)
        write_kt(kt0_ref, ktn[:GROUP_W], 0, r)
        ktb_ref[:, r] = ktn[GROUP_W:].astype(BF16)

    def permute(d):
        n = TM // d
        for r in range(d):
            for c in range(n_slabs):
                hperm_ref[r * n:(r + 1) * n, c * LANES:(c + 1) * LANES] = (
                    slab_ref[c, pl.ds(r, n, stride=d), :].astype(BF16))

    permute(DILATIONS[1])
    pg = jnp.dot(hperm_ref[...], wg1_ref[...], preferred_element_type=F32)
    put_slabs(a1_ref, 0, rope_cols(pg, 0, 1, scale))
    put_slabs(a1_ref, n_pairs, pg[:, GROUP_W:])
    kt1 = lax.dot_general(wkg1_ref[...], hperm_ref[...], nt_dims, preferred_element_type=F32)
    write_kt(kt1_ref, kt1, 1)

    permute(DILATIONS[2])
    pg = jnp.dot(hperm_ref[...], wg2_ref[...], preferred_element_type=F32)
    put_slabs(a2_ref, 0, rope_cols(pg, 0, 2, scale))
    k2_a, k2_b = head_masked(rope_cols(pg, GROUP_W, 2, 1.0))
    put_slabs(a2_ref, n_pairs, k2_a)
    put_slabs(a2_ref, 2 * n_pairs, k2_b)
    put_slabs(a2_ref, 3 * n_pairs, pg[:, 2 * GROUP_W:])


def _resident(stacked_shape, layer):
    nd = len(stacked_shape) - 1
    return pl.BlockSpec((None,) + tuple(stacked_shape[1:]), lambda *_: (layer,) + (0,) * nd,
                        pipeline_mode=pl.Buffered(1))


def _ffn_proj(layer, x, g1, wup, wdown, g2, wnat, wknat, wg1, wkg1, wg2, tab, tab_t):
    b = x.shape[0]
    nt = SEQ // TM
    n_pairs = GROUP_W // LANES
    nb_pairs = NA_W // LANES
    row = pl.BlockSpec((None, TM, D_MODEL), lambda i, j: (i, j, 0))
    slabs = lambda n: pl.BlockSpec((None, n, TM, LANES), lambda i, j: (i, 0, j, 0))
    col = lambda h: pl.BlockSpec((None, h, TM), lambda i, j: (i, 0, j))
    slab_shape = lambda n: jax.ShapeDtypeStruct((b, n, SEQ, LANES), BF16)
    out_shape = (
        jax.ShapeDtypeStruct((b, SEQ, D_MODEL), F32),
        slab_shape(2 * n_pairs),
        jax.ShapeDtypeStruct((b, GROUP_W, SEQ), BF16),
        slab_shape(2 * n_pairs),
        jax.ShapeDtypeStruct((b, GROUP_W, SEQ), BF16),
        slab_shape(4 * n_pairs),
        slab_shape(3 * nb_pairs),
        jax.ShapeDtypeStruct((b, NA_W, SEQ), BF16),
    )
    weights = (g1, wup, wdown, g2, wnat, wknat, wg1, wkg1, wg2)
    return pl.pallas_call(
        _ffn_proj_kernel,
        grid=(b, nt),
        in_specs=[row] + [_resident(w.shape, layer) for w in weights]
        + [pl.BlockSpec((TM, tab.shape[1]), lambda i, j: (j, 0)), pl.BlockSpec((tab_t.shape[0], TM), lambda i, j: (0, j))],
        out_specs=(row, slabs(2 * n_pairs), col(GROUP_W), slabs(2 * n_pairs), col(GROUP_W),
                   slabs(4 * n_pairs), slabs(3 * nb_pairs), col(NA_W)),
        out_shape=out_shape,
        scratch_shapes=[pltpu.VMEM((D_MODEL // LANES, TM, LANES), F32), pltpu.VMEM((TM, D_MODEL), BF16)],
        compiler_params=pltpu.CompilerParams(dimension_semantics=("parallel", "parallel"),
                                             vmem_limit_bytes=VMEM_LIMIT_BYTES),
        name="ffn_proj",
    )(x, *weights, tab, tab_t)


def _stacked_scores(q_stack, keys, bias):
    return [jnp.dot(q_stack, keys, preferred_element_type=F32) + bias]


def _split_scores(q, keys, bias):
    out = []
    for h in range(2):
        if isinstance(keys, (tuple, list)):
            s = lax.dot_general(q, keys[h], (((1,), (1,)), ((), ())), preferred_element_type=F32)
        else:
            zeros = jnp.zeros((HEAD_DIM, keys.shape[1]), BF16)
            own = keys[h * HEAD_DIM:(h + 1) * HEAD_DIM]
            s = jnp.dot(q, jnp.concatenate([own, zeros] if h == 0 else [zeros, own], axis=0),
                        preferred_element_type=F32)
        out.append(s + bias)
    return out


def _pair_finish(scores, v):
    vx = jnp.concatenate([v, jnp.ones(v.shape, BF16)], axis=1)
    res = []
    for sc in scores:
        m = jnp.max(sc, axis=-1, keepdims=True)
        p = jnp.exp(sc - m).astype(BF16)
        res.append((jnp.dot(p, vx, preferred_element_type=F32), m))
    if len(res) == 1:
        half = res[0][0].shape[0] // 2
        res = [(res[0][0][h * half:(h + 1) * half], res[0][1][h * half:(h + 1) * half]) for h in range(2)]
    shape = (res[0][0].shape[0], LANES)
    left = lax.broadcasted_iota(jnp.int32, shape, 1) < HEAD_DIM
    acc = jnp.where(left, res[0][0][:, :LANES], res[1][0][:, :LANES])
    l = jnp.where(left, res[0][0][:, LANES:], res[1][0][:, LANES:])
    m = jnp.where(left, jnp.broadcast_to(res[0][1], shape), jnp.broadcast_to(res[1][1], shape))
    return acc, m, l


def _software_pipeline(blocks, depth):
    pending = []
    for score_fn, finish_fn in blocks:
        pending.append((finish_fn, score_fn()))
        if len(pending) > depth:
            fn, sc = pending.pop(0)
            fn(sc)
    for fn, sc in pending:
        fn(sc)


CHUNK = 2 * BAND


def _band_bias(m, n, lo, hi):
    u = lax.broadcasted_iota(jnp.int32, (m, n), 0)
    w = lax.broadcasted_iota(jnp.int32, (m, n), 1)
    d = w - u
    return jnp.where(d >= lo, jnp.where(d <= hi, 0.0, NEG_INF), NEG_INF).astype(F32)


def _dilated_kernel(a0_ref, kt0_full, a1_ref, kt1_full, a2_ref, o_full, out_full, lse_full):
    n_pairs = GROUP_W // LANES
    blocks = []
    for p in range(n_pairs):
        _dilated_pair(blocks, a0_ref.at[p], a0_ref.at[n_pairs + p], kt0_full.at[p * LANES:(p + 1) * LANES],
                      a1_ref.at[p], a1_ref.at[n_pairs + p], kt1_full.at[p * LANES:(p + 1) * LANES],
                      a2_ref.at[p], a2_ref.at[n_pairs + p], a2_ref.at[2 * n_pairs + p], a2_ref.at[3 * n_pairs + p],
                      o_full.at[p], out_full.at[p], lse_full.at[p])
    _software_pipeline(blocks, DIL_PIPE_DEPTH)


def _dilated_pair(blocks, q0_ref, v0_ref, kt0_ref, q1_ref, v1_ref, kt1_ref, q2_ref, k2a_ref, k2b_ref, v2_ref,
                  o_ref, out_s, lse_s):
    q_refs = (q0_ref, q1_ref, q2_ref)
    v_refs = (v0_ref, v1_ref, v2_ref)

    bias_first = _band_bias(BAND, CHUNK, -BAND, BAND)
    bias_last = _band_bias(BAND, CHUNK, 0, CHUNK)
    bias_full = _band_bias(CHUNK, 2 * CHUNK, 0, CHUNK)
    bias_one = _band_bias(CHUNK, CHUNK, -BAND, BAND)

    def normalise(vals):
        acc, m, l = vals
        return acc / l, m + jnp.log(l)

    def stash(g, outs, vals):
        o, lse = normalise(vals)
        for row0, n, dst, stride in outs:
            idx = pl.ds(dst, n, stride=stride)
            out_s[g - 1, idx, :] = o[row0:row0 + n]
            lse_s[g - 1, idx, :] = lse[row0:row0 + n]

    def merge(g, outs, vals):
        assert g == 0
        o0, lse0 = normalise(vals)
        for row0, n, dst, stride in outs:
            assert stride == 1
            sl = slice(dst, dst + n)
            os_ = [o0[row0:row0 + n], out_s[0, sl, :], out_s[1, sl, :]]
            ls_ = [lse0[row0:row0 + n], lse_s[0, sl, :], lse_s[1, sl, :]]
            mx = jnp.maximum(jnp.maximum(ls_[0], ls_[1]), ls_[2])
            ws = [jnp.exp(x - mx) for x in ls_]
            num = ws[0] * os_[0] + ws[1] * os_[1] + ws[2] * os_[2]
            o_ref[sl, :] = (num / (ws[0] + ws[1] + ws[2])).astype(BF16)

    def add(g, q_slices, keys_fn, v_slices, bias, outs):
        def score_fn():
            q = jnp.concatenate([q_refs[g][sl, :] for sl in q_slices], axis=0)
            return _split_scores(q, keys_fn(), bias)

        def finish_fn(scores):
            v = jnp.concatenate([v_refs[g][sl, :] for sl in v_slices], axis=0)
            (merge if g == 0 else stash)(g, outs, _pair_finish(scores, v))

        blocks.append((score_fn, finish_fn))

    def class_blocks(g, kt_ref, base, nat, stride):
        nc = len(base)
        rows = lambda j, lo, n: slice(base[j] + lo, base[j] + lo + n)
        add(g, [rows(0, 0, BAND)], lambda: kt_ref[:, rows(0, 0, CHUNK)], [rows(0, 0, CHUNK)],
            bias_first, [(0, BAND, nat[0], stride)])
        for j in range(nc - 1):
            add(g, [rows(j, BAND, BAND), rows(j + 1, 0, BAND)],
                lambda j=j: jnp.concatenate([kt_ref[:, rows(j, 0, CHUNK)], kt_ref[:, rows(j + 1, 0, CHUNK)]], axis=1),
                [rows(j, 0, CHUNK), rows(j + 1, 0, CHUNK)],
                bias_full, [(0, BAND, nat[j] + BAND * stride, stride), (BAND, BAND, nat[j + 1], stride)])
        add(g, [rows(nc - 1, BAND, BAND)], lambda: kt_ref[:, rows(nc - 1, 0, CHUNK)],
            [rows(nc - 1, 0, CHUNK)], bias_last, [(0, BAND, nat[nc - 1] + BAND * stride, stride)])

    nt = SEQ // TM
    d2 = DILATIONS[2]
    piece = TM // d2
    for r in range(d2):
        pieces = [slice(j * TM + r * piece, j * TM + (r + 1) * piece) for j in range(nt)]
        add(2, pieces,
            lambda pieces=pieces: [jnp.concatenate([k_ref[sl, :] for sl in pieces], axis=0) for k_ref in (k2a_ref, k2b_ref)],
            pieces, bias_one, [(j * piece, piece, j * TM + r, d2) for j in range(nt)])

    d1 = DILATIONS[1]
    for r in range(d1):
        class_blocks(1, kt1_ref, [j * TM + r * CHUNK for j in range(nt)], [j * TM + r for j in range(nt)], d1)

    chunk_rows = [c * CHUNK for c in range(SEQ // CHUNK)]
    class_blocks(0, kt0_ref, chunk_rows, chunk_rows, 1)


def _dilated(a0, kt0, a1, kt1, a2):
    b = a0.shape[0]
    n_pairs = GROUP_W // LANES
    slabs = lambda n: pl.BlockSpec((None, n, SEQ, LANES), lambda i: (i, 0, 0, 0))
    ktspec = pl.BlockSpec((None, GROUP_W, SEQ), lambda i: (i, 0, 0))
    return pl.pallas_call(
        _dilated_kernel,
        grid=(b,),
        in_specs=[slabs(2 * n_pairs), ktspec, slabs(2 * n_pairs), ktspec, slabs(4 * n_pairs)],
        out_specs=slabs(n_pairs),
        out_shape=jax.ShapeDtypeStruct((b, n_pairs, SEQ, LANES), BF16),
        scratch_shapes=[pltpu.VMEM((n_pairs, 2, SEQ, LANES), F32), pltpu.VMEM((n_pairs, 2, SEQ, LANES), F32)],
        compiler_params=pltpu.CompilerParams(dimension_semantics=("parallel",),
                                             vmem_limit_bytes=VMEM_LIMIT_BYTES),
        name="dilated",
    )(a0, kt0, a1, kt1, a2)


def _na_window_start(i):
    return int(np.clip(i - NA_ROWS // 2, 0, GRID_ROWS - NA_ROWS))


def _na_kernel(nb_ref, kt_full, bias_full, o_full, kts_full):
    blocks = []
    for p in range(NA_PAIRS_PER_STEP):
        _na_pair(blocks, nb_ref.at[0, p], nb_ref.at[1, p], nb_ref.at[2, p],
                 kt_full.at[p * LANES:(p + 1) * LANES], bias_full.at[p], o_full.at[p], kts_full.at[p])
    _software_pipeline(blocks, NA_PIPE_DEPTH)


def _na_pair(blocks, qa_ref, qb_ref, v_ref, kt_ref, bias_ref, o_ref, kts_ref):
    words = pltpu.bitcast(kt_ref[...], jnp.uint32)
    kts_ref[...] = pltpu.bitcast(pltpu.roll(words, SEQ - GRID_W, 1), BF16)

    for i in range(GRID_ROWS):
        lo = _na_window_start(i)
        case = NA_CASE_ROWS.index(i) if i in NA_CASE_ROWS else NA_CASE_ROWS.index(NA_ROWS // 2)
        qsl = slice(i * GRID_W, (i + 1) * GRID_W)
        vsl = slice(lo * GRID_W, lo * GRID_W + NA_KEYS)
        if lo % 2 == 0:
            keys_fn = lambda lo=lo: kt_ref[:, lo * GRID_W:lo * GRID_W + NA_KEYS]
        else:
            keys_fn = lambda lo=lo: kts_ref[:, (lo - 1) * GRID_W:(lo - 1) * GRID_W + NA_KEYS]

        def score_fn(qsl=qsl, keys_fn=keys_fn, case=case):
            q_stack = jnp.concatenate([qa_ref[qsl, :], qb_ref[qsl, :]], axis=0)
            return _stacked_scores(q_stack, keys_fn(), bias_ref[case])

        def finish_fn(scores, qsl=qsl, vsl=vsl):
            acc, _, l = _pair_finish(scores, v_ref[vsl, :])
            o_ref[qsl, :] = (acc / l).astype(BF16)

        blocks.append((score_fn, finish_fn))


def _nbr(layer, nb, ktb, bias_tab):
    b = nb.shape[0]
    n_pairs = NA_W // LANES
    pps = NA_PAIRS_PER_STEP
    nb_view = nb.reshape(b, 3, n_pairs // pps, pps, SEQ, LANES)
    return pl.pallas_call(
        _na_kernel,
        grid=(n_pairs // pps, b),
        in_specs=[pl.BlockSpec((None, 3, None, pps, SEQ, LANES), lambda g, i: (i, 0, g, 0, 0, 0)),
                  pl.BlockSpec((None, pps * LANES, SEQ), lambda g, i: (i, g, 0)),
                  pl.BlockSpec((None, pps, len(NA_CASE_ROWS), 2 * GRID_W, NA_KEYS), lambda g, i: (layer, g, 0, 0, 0))],
        out_specs=pl.BlockSpec((None, pps, SEQ, LANES), lambda g, i: (i, g, 0, 0)),
        out_shape=jax.ShapeDtypeStruct((b, n_pairs, SEQ, LANES), BF16),
        scratch_shapes=[pltpu.VMEM((pps, LANES, SEQ), BF16)],
        compiler_params=pltpu.CompilerParams(dimension_semantics=("parallel", "parallel"),
                                             vmem_limit_bytes=VMEM_LIMIT_BYTES),
        name="nbr",
    )(nb_view, ktb, bias_tab)


def _bias_table_kernel(rb_ref, o_ref):
    n = GRID_W
    qc = lax.broadcasted_iota(jnp.int32, (n, LANES), 0)
    lane = lax.broadcasted_iota(jnp.int32, (n, LANES), 1)
    kc = lane % n
    win_lo = jnp.clip(qc - NA_COLS // 2, 0, n - NA_COLS)
    windowed = lambda t: jnp.where(kc >= win_lo, jnp.where(kc < win_lo + NA_COLS, t, NEG_INF), NEG_INF)
    low = lane < n
    placed = []
    for i in range(2 * NA_ROWS - 1):
        v = jnp.broadcast_to(rb_ref[i:i + 1, :], (n, LANES))
        even = pltpu.roll(v, n + 1, 1, stride=1, stride_axis=0)
        odd = pltpu.roll(v, 1, 1, stride=1, stride_axis=0)
        placed.append((windowed(even), windowed(odd)))
    for c, i in enumerate(NA_CASE_ROWS):
        first = _na_window_start(i) - i + NA_ROWS - 1
        for t in range(NA_ROWS // 2):
            o_ref[c, :, t * LANES:(t + 1) * LANES] = jnp.where(low, placed[first + 2 * t][0],
                                                               placed[first + 2 * t + 1][1])


def _na_bias_table(rel_bias):
    nl = rel_bias.shape[0]
    n_dr = 2 * NA_ROWS - 1
    nc = len(NA_CASE_ROWS)
    pad_lo = (GRID_W - 1) - (NA_COLS - 1)
    rows = jnp.pad(rel_bias.astype(F32), ((0, 0), (0, 0), (0, 0), (pad_lo, LANES - (2 * NA_COLS - 1) - pad_lo)))
    return pl.pallas_call(
        _bias_table_kernel,
        grid=(nl, NA_HEADS),
        in_specs=[pl.BlockSpec((None, None, n_dr, LANES), lambda l, h: (l, h, 0, 0))],
        out_specs=pl.BlockSpec((None, None, nc, GRID_W, NA_KEYS), lambda l, h: (l, h // 2, 0, h % 2, 0)),
        out_shape=jax.ShapeDtypeStruct((nl, NA_HEADS // 2, nc, 2 * GRID_W, NA_KEYS), F32),
        compiler_params=pltpu.CompilerParams(dimension_semantics=("parallel", "parallel")),
        name="bias_table",
    )(rows)


def _out_ffn_kernel(x1_ref, ya_ref, yb_ref, gm_ref, wgate_ref, wa_ref, wb_ref, wout_ref,
                    g3_ref, wup_ref, wdown_ref, gf_ref, o_ref, *, final):
    n_split = TM_OUT // SUB_ROWS
    parts = [slice(i * SUB_ROWS, (i + 1) * SUB_ROWS) for i in range(n_split)]
    x1s, merged = [], []
    for r in parts:
        x1 = x1_ref[r, :]
        h = _rms(x1, gm_ref[...]).astype(BF16)
        gates = jax.nn.sigmoid(jnp.dot(h, wgate_ref[...], preferred_element_type=F32))
        ya = jnp.concatenate([ya_ref[c, r, :] for c in range(ya_ref.shape[0])], axis=1)
        yb = jnp.concatenate([yb_ref[c, r, :] for c in range(yb_ref.shape[0])], axis=1)
        ba = jnp.dot(ya, wa_ref[...], preferred_element_type=F32)
        bb = jnp.dot(yb, wb_ref[...], preferred_element_type=F32)
        x1s.append(x1)
        merged.append((gates[:, :D_MODEL] * ba + gates[:, D_MODEL:] * bb).astype(BF16))
    x2s = [x1 + jnp.dot(m, wout_ref[...], preferred_element_type=F32) for x1, m in zip(x1s, merged)]
    h3s = [_rms(x2, g3_ref[...]).astype(BF16) for x2 in x2s]
    for r, x2, h3 in zip(parts, x2s, h3s):
        x3 = x2 + 0.5 * _swiglu_half(h3, wup_ref, wdown_ref)
        if final:
            x3 = _rms(x3, gf_ref[...])
        o_ref[r, :] = x3


def _out_ffn(layer, x1, ya, yb, gm, wgate, wa, wb, wout, g3, wup, wdown, gf, final):
    b = x1.shape[0]
    row = pl.BlockSpec((None, TM_OUT, D_MODEL), lambda i, j: (i, j, 0))
    slabs = lambda n: pl.BlockSpec((None, n, TM_OUT, LANES), lambda i, j: (i, 0, j, 0))
    weights = (gm, wgate, wa, wb, wout, g3, wup, wdown)
    gf_spec = pl.BlockSpec(gf.shape, lambda i, j: (0, 0))
    return pl.pallas_call(
        functools.partial(_out_ffn_kernel, final=final),
        grid=(b, SEQ // TM_OUT),
        in_specs=[row, slabs(ya.shape[1]), slabs(yb.shape[1])] + [_resident(w.shape, layer) for w in weights] + [gf_spec],
        out_specs=row,
        out_shape=jax.ShapeDtypeStruct((b, SEQ, D_MODEL), F32),
        compiler_params=pltpu.CompilerParams(dimension_semantics=("parallel", "parallel"),
                                             vmem_limit_bytes=VMEM_LIMIT_BYTES),
        name="out_ffn",
    )(x1, ya, yb, *weights, gf)


def _rope_tables():
    half = HEAD_DIM // 2
    pos = jnp.arange(SEQ)
    inv_freq = ROPE_THETA ** (-jnp.arange(half, dtype=F32) / half)
    ang = pos.astype(F32)[:, None] * inv_freq[None, :]
    cos = jnp.cos(ang)
    sin = jnp.sin(ang)
    cos_slab = jnp.tile(cos, (1, LANES // half))
    sin_slab = jnp.tile(jnp.concatenate([-sin, sin], axis=-1), (1, LANES // HEAD_DIM))

    def class_major(t, d):
        w = t.shape[-1]
        return t.reshape(SEQ // TM, TM // d, d, w).transpose(0, 2, 1, 3).reshape(SEQ, w)

    slabs = [cos_slab, sin_slab]
    for d in DILATIONS[1:]:
        slabs += [class_major(cos_slab, d), class_major(sin_slab, d)]
    d1 = DILATIONS[1]
    rows = [cos.T, sin.T, class_major(cos, d1).T, class_major(sin, d1).T]
    return jnp.concatenate(slabs, axis=1), jnp.concatenate(rows, axis=0)


def kernel(x, ffn1_norm, ffn1_w_up, ffn1_w_down, mix_norm, w_in, na_rel_bias, w_branch_a, w_branch_b,
           w_out, ffn2_norm, ffn2_w_up, ffn2_w_down, final_norm):
    assert x.shape[1:] == (SEQ, D_MODEL) and x.dtype == F32
    tab, tab_t = _rope_tables()
    gain = lambda g: g.astype(F32).reshape(-1, 1, D_MODEL)
    w = w_in.astype(BF16)
    aq, ak, av = (w[:, :, i * 3 * GROUP_W:(i + 1) * 3 * GROUP_W] for i in range(3))
    bq, bk, bv = (w[:, :, DIL_QKV + i * NA_W:DIL_QKV + (i + 1) * NA_W] for i in range(3))
    grp = lambda t, g: t[:, :, g * GROUP_W:(g + 1) * GROUP_W]
    wnat = jnp.concatenate([grp(aq, 0), grp(av, 0), bq, bv], axis=2)
    wknat = jnp.concatenate([grp(ak, 0), bk], axis=2).transpose(0, 2, 1)
    wg1 = jnp.concatenate([grp(aq, 1), grp(av, 1)], axis=2)
    wkg1 = grp(ak, 1).transpose(0, 2, 1)
    wg2 = jnp.concatenate([grp(aq, 2), grp(ak, 2), grp(av, 2)], axis=2)
    wgate = w[:, :, DIL_QKV + NA_QKV:]
    g_ffn1, g_mix, g_ffn2 = gain(ffn1_norm), gain(mix_norm), gain(ffn2_norm)
    up1, down1 = ffn1_w_up.astype(BF16), ffn1_w_down.astype(BF16)
    up2, down2 = ffn2_w_up.astype(BF16), ffn2_w_down.astype(BF16)
    wa, wb, wo = w_branch_a.astype(BF16), w_branch_b.astype(BF16), w_out.astype(BF16)
    bias_tab = _na_bias_table(na_rel_bias)
    g_final = final_norm.astype(F32).reshape(1, D_MODEL)

    for layer in range(DEPTH):
        x1, a0, kt0, a1, kt1, a2, nb, ktb = _ffn_proj(
            layer, x, g_ffn1, up1, down1, g_mix, wnat, wknat, wg1, wkg1, wg2, tab, tab_t)
        ya = _dilated(a0, kt0, a1, kt1, a2)
        yb = _nbr(layer, nb, ktb, bias_tab)
        x = _out_ffn(layer, x1, ya, yb, g_mix, wgate, wa, wb, wo, g_ffn2, up2, down2, g_final,
                     final=(layer == DEPTH - 1))
    return x
```

```python
import functools

import numpy as np
import jax
import jax.numpy as jnp
from jax import lax
from jax.experimental import pallas as pl
from jax.experimental.pallas import tpu as pltpu

F32 = jnp.float32
BF16 = jnp.bfloat16

D_MODEL = 1024
SEQ = 2048
DEPTH = 2
HEAD_DIM = 64
DILATIONS = (1, 4, 16)
BAND = 64
GROUP_HEADS = 4
GROUP_W = GROUP_HEADS * HEAD_DIM
NA_HEADS = 8
NA_W = NA_HEADS * HEAD_DIM
GRID_W = 64
GRID_ROWS = SEQ // GRID_W
NA_ROWS = 8
NA_COLS = 16
D_FF = 2816
ROPE_THETA = 10000.0
RMS_EPS = 1e-6
NEG_INF = -1e30
DIL_QKV = 3 * 3 * GROUP_W
NA_QKV = 3 * NA_W

LANES = 128
VMEM_LIMIT_BYTES = 56 * 1024 * 1024

TM = 512
MXU_TILE = 256
TM_OUT = 1024
SUB_ROWS = 256
FF_CHUNKS = (6 * MXU_TILE, 5 * MXU_TILE)
assert sum(FF_CHUNKS) == D_FF
NA_KEYS = NA_ROWS * GRID_W
DIL_PIPE_DEPTH = 4
NA_PIPE_DEPTH = 3
NA_PAIRS_PER_STEP = 4
NA_CASE_ROWS = (0, 1, 2, 3, 4, 29, 30, 31)


def _rms(x, g):
    return x * lax.rsqrt(jnp.mean(x * x, axis=-1, keepdims=True) + RMS_EPS) * g


def _sigmoid(t):
    return 0.5 * jnp.tanh(0.5 * t) + 0.5


def _swiglu_half(h, w_up_ref, w_down_ref):
    acc = None
    lo = 0
    for width in FF_CHUNKS:
        g = jnp.dot(h, w_up_ref[:, lo:lo + width], preferred_element_type=F32)
        u = jnp.dot(h, w_up_ref[:, D_FF + lo:D_FF + lo + width], preferred_element_type=F32)
        a = (g * _sigmoid(g) * u).astype(BF16)
        part = jnp.dot(a, w_down_ref[lo:lo + width, :], preferred_element_type=F32)
        acc = part if acc is None else acc + part
        lo += width
    return acc


def _rope(t, cos, sin_signed):
    lane = lax.broadcasted_iota(jnp.int32, t.shape, 1)
    first_half = (lane % HEAD_DIM) < (HEAD_DIM // 2)
    partner = jnp.where(first_half, pltpu.roll(t, LANES - HEAD_DIM // 2, 1), pltpu.roll(t, HEAD_DIM // 2, 1))
    return t * cos + partner * sin_signed


def _ffn_proj_kernel(x_ref, g1_ref, wup_ref, wdown_ref, g2_ref, wnat_ref, wknat_ref, wg1_ref, wkg1_ref, wg2_ref,
                     tab_ref, tabt_ref,
                     x1_ref, a0_ref, kt0_ref, a1_ref, kt1_ref, a2_ref, nb_ref, ktb_ref,
                     slab_ref, hperm_ref):
    parts = [slice(i * SUB_ROWS, (i + 1) * SUB_ROWS) for i in range(TM // SUB_ROWS)]
    n_slabs = D_MODEL // LANES
    h2bs = []
    for r in parts:
        x = x_ref[r, :]
        h = _rms(x, g1_ref[...]).astype(BF16)
        x1 = x + 0.5 * _swiglu_half(h, wup_ref, wdown_ref)
        x1_ref[r, :] = x1
        h2 = _rms(x1, g2_ref[...])
        for c in range(n_slabs):
            slab_ref[c, r, :] = h2[:, c * LANES:(c + 1) * LANES]
        h2bs.append(h2.astype(BF16))

    scale = HEAD_DIM ** -0.5
    half = HEAD_DIM // 2
    nt_dims = (((1,), (1,)), ((), ()))

    def tables(order, r):
        cos = tab_ref[r, (2 * order) * LANES:(2 * order + 1) * LANES]
        sin = tab_ref[r, (2 * order + 1) * LANES:(2 * order + 2) * LANES]
        return cos, sin

    def put_slabs(a_ref, first, t, r=slice(None)):
        for c in range(t.shape[1] // LANES):
            a_ref[first + c, r, :] = t[:, c * LANES:(c + 1) * LANES].astype(BF16)

    def head_masked(t):
        lane = lax.broadcasted_iota(jnp.int32, t.shape, 1)
        first = (lane % LANES) < HEAD_DIM
        return jnp.where(first, t, 0.0), jnp.where(first, 0.0, t)

    def rope_cols(p, lo, order, mul, r=slice(None)):
        cos, sin = tables(order, r)
        return jnp.concatenate([_rope(p[:, lo + c * LANES:lo + (c + 1) * LANES] * mul, cos, sin)
                                for c in range(GROUP_W // LANES)], axis=1)

    def write_kt(kt_ref, kt, order, r=slice(None)):
        cos = tabt_ref[(2 * order) * half:(2 * order + 1) * half, r]
        sin = tabt_ref[(2 * order + 1) * half:(2 * order + 2) * half, r]
        for hh in range(GROUP_HEADS):
            t1 = kt[hh * HEAD_DIM:hh * HEAD_DIM + half, :]
            t2 = kt[hh * HEAD_DIM + half:(hh + 1) * HEAD_DIM, :]
            kt_ref[hh * HEAD_DIM:hh * HEAD_DIM + half, r] = (t1 * cos - t2 * sin).astype(BF16)
            kt_ref[hh * HEAD_DIM + half:(hh + 1) * HEAD_DIM, r] = (t2 * cos + t1 * sin).astype(BF16)

    n_pairs = GROUP_W // LANES
    nb_pairs = NA_W // LANES

    for r, h2b in zip(parts, h2bs):
        pn = jnp.dot(h2b, wnat_ref[...], preferred_element_type=F32)
        put_slabs(a0_ref, 0, rope_cols(pn, 0, 0, scale, r), r)
        put_slabs(a0_ref, n_pairs, pn[:, GROUP_W:2 * GROUP_W], r)
        qb_a, qb_b = head_masked(pn[:, 2 * GROUP_W:2 * GROUP_W + NA_W] * scale)
        put_slabs(nb_ref, 0, qb_a, r)
        put_slabs(nb_ref, nb_pairs, qb_b, r)
        put_slabs(nb_ref, 2 * nb_pairs, pn[:, 2 * GROUP_W + NA_W:], r)
        ktn = lax.dot_general(wknat_ref[...], h2b, nt_dims, preferred_element_type=F32)
        write_kt(kt0_ref, ktn[:GROUP_W], 0, r)
        ktb_ref[:, r] = ktn[GROUP_W:].astype(BF16)

    def permute(d):
        n = TM // d
        for r in range(d):
            for c in range(n_slabs):
                hperm_ref[r * n:(r + 1) * n, c * LANES:(c + 1) * LANES] = (
                    slab_ref[c, pl.ds(r, n, stride=d), :].astype(BF16))

    permute(DILATIONS[1])
    pg = jnp.dot(hperm_ref[...], wg1_ref[...], preferred_element_type=F32)
    put_slabs(a1_ref, 0, rope_cols(pg, 0, 1, scale))
    put_slabs(a1_ref, n_pairs, pg[:, GROUP_W:])
    kt1 = lax.dot_general(wkg1_ref[...], hperm_ref[...], nt_dims, preferred_element_type=F32)
    write_kt(kt1_ref, kt1, 1)

    permute(DILATIONS[2])
    pg = jnp.dot(hperm_ref[...], wg2_ref[...], preferred_element_type=F32)
    put_slabs(a2_ref, 0, rope_cols(pg, 0, 2, scale))
    k2_a, k2_b = head_masked(rope_cols(pg, GROUP_W, 2, 1.0))
    put_slabs(a2_ref, n_pairs, k2_a)
    put_slabs(a2_ref, 2 * n_pairs, k2_b)
    put_slabs(a2_ref, 3 * n_pairs, pg[:, 2 * GROUP_W:])


def _resident(stacked_shape, layer):
    nd = len(stacked_shape) - 1
    return pl.BlockSpec((None,) + tuple(stacked_shape[1:]), lambda *_: (layer,) + (0,) * nd,
                        pipeline_mode=pl.Buffered(1))


def _ffn_proj(layer, x, g1, wup, wdown, g2, wnat, wknat, wg1, wkg1, wg2, tab, tab_t):
    b = x.shape[0]
    nt = SEQ // TM
    n_pairs = GROUP_W // LANES
    nb_pairs = NA_W // LANES
    row = pl.BlockSpec((None, TM, D_MODEL), lambda i, j: (i, j, 0))
    slabs = lambda n: pl.BlockSpec((None, n, TM, LANES), lambda i, j: (i, 0, j, 0))
    col = lambda h: pl.BlockSpec((None, h, TM), lambda i, j: (i, 0, j))
    slab_shape = lambda n: jax.ShapeDtypeStruct((b, n, SEQ, LANES), BF16)
    out_shape = (
        jax.ShapeDtypeStruct((b, SEQ, D_MODEL), F32),
        slab_shape(2 * n_pairs),
        jax.ShapeDtypeStruct((b, GROUP_W, SEQ), BF16),
        slab_shape(2 * n_pairs),
        jax.ShapeDtypeStruct((b, GROUP_W, SEQ), BF16),
        slab_shape(4 * n_pairs),
        slab_shape(3 * nb_pairs),
        jax.ShapeDtypeStruct((b, NA_W, SEQ), BF16),
    )
    weights = (g1, wup, wdown, g2, wnat, wknat, wg1, wkg1, wg2)
    return pl.pallas_call(
        _ffn_proj_kernel,
        grid=(b, nt),
        in_specs=[row] + [_resident(w.shape, layer) for w in weights]
        + [pl.BlockSpec((TM, tab.shape[1]), lambda i, j: (j, 0)), pl.BlockSpec((tab_t.shape[0], TM), lambda i, j: (0, j))],
        out_specs=(row, slabs(2 * n_pairs), col(GROUP_W), slabs(2 * n_pairs), col(GROUP_W),
                   slabs(4 * n_pairs), slabs(3 * nb_pairs), col(NA_W)),
        out_shape=out_shape,
        scratch_shapes=[pltpu.VMEM((D_MODEL // LANES, TM, LANES), F32), pltpu.VMEM((TM, D_MODEL), BF16)],
        compiler_params=pltpu.CompilerParams(dimension_semantics=("parallel", "parallel"),
                                             vmem_limit_bytes=VMEM_LIMIT_BYTES),
        name="ffn_proj",
    )(x, *weights, tab, tab_t)


def _stacked_scores(q_stack, keys, bias):
    return [jnp.dot(q_stack, keys, preferred_element_type=F32) + bias]


def _split_scores(q, keys, bias):
    out = []
    for h in range(2):
        if isinstance(keys, (tuple, list)):
            s = lax.dot_general(q, keys[h], (((1,), (1,)), ((), ())), preferred_element_type=F32)
        else:
            zeros = jnp.zeros((HEAD_DIM, keys.shape[1]), BF16)
            own = keys[h * HEAD_DIM:(h + 1) * HEAD_DIM]
            s = jnp.dot(q, jnp.concatenate([own, zeros] if h == 0 else [zeros, own], axis=0),
                        preferred_element_type=F32)
        out.append(s + bias)
    return out


def _pair_finish(scores, v):
    vx = jnp.concatenate([v, jnp.ones(v.shape, BF16)], axis=1)
    res = []
    for sc in scores:
        m = jnp.max(sc, axis=-1, keepdims=True)
        p = jnp.exp(sc - m).astype(BF16)
        res.append((jnp.dot(p, vx, preferred_element_type=F32), m))
    if len(res) == 1:
        half = res[0][0].shape[0] // 2
        res = [(res[0][0][h * half:(h + 1) * half], res[0][1][h * half:(h + 1) * half]) for h in range(2)]
    shape = (res[0][0].shape[0], LANES)
    left = lax.broadcasted_iota(jnp.int32, shape, 1) < HEAD_DIM
    acc = jnp.where(left, res[0][0][:, :LANES], res[1][0][:, :LANES])
    l = jnp.where(left, res[0][0][:, LANES:], res[1][0][:, LANES:])
    m = jnp.where(left, jnp.broadcast_to(res[0][1], shape), jnp.broadcast_to(res[1][1], shape))
    return acc, m, l


def _software_pipeline(blocks, depth):
    pending = []
    for score_fn, finish_fn in blocks:
        pending.append((finish_fn, score_fn()))
        if len(pending) > depth:
            fn, sc = pending.pop(0)
            fn(sc)
    for fn, sc in pending:
        fn(sc)


CHUNK = 2 * BAND


def _band_bias(m, n, lo, hi):
    u = lax.broadcasted_iota(jnp.int32, (m, n), 0)
    w = lax.broadcasted_iota(jnp.int32, (m, n), 1)
    d = w - u
    return jnp.where(d >= lo, jnp.where(d <= hi, 0.0, NEG_INF), NEG_INF).astype(F32)


def _dilated_kernel(a0_ref, kt0_full, a1_ref, kt1_full, a2_ref, o_full, out_full, lse_full):
    n_pairs = GROUP_W // LANES
    blocks = []
    for p in range(n_pairs):
        _dilated_pair(blocks, a0_ref.at[p], a0_ref.at[n_pairs + p], kt0_full.at[p * LANES:(p + 1) * LANES],
                      a1_ref.at[p], a1_ref.at[n_pairs + p], kt1_full.at[p * LANES:(p + 1) * LANES],
                      a2_ref.at[p], a2_ref.at[n_pairs + p], a2_ref.at[2 * n_pairs + p], a2_ref.at[3 * n_pairs + p],
                      o_full.at[p], out_full.at[p], lse_full.at[p])
    _software_pipeline(blocks, DIL_PIPE_DEPTH)


def _dilated_pair(blocks, q0_ref, v0_ref, kt0_ref, q1_ref, v1_ref, kt1_ref, q2_ref, k2a_ref, k2b_ref, v2_ref,
                  o_ref, out_s, lse_s):
    q_refs = (q0_ref, q1_ref, q2_ref)
    v_refs = (v0_ref, v1_ref, v2_ref)

    bias_first = _band_bias(BAND, CHUNK, -BAND, BAND)
    bias_last = _band_bias(BAND, CHUNK, 0, CHUNK)
    bias_full = _band_bias(CHUNK, 2 * CHUNK, 0, CHUNK)
    bias_one = _band_bias(CHUNK, CHUNK, -BAND, BAND)

    def normalise(vals):
        acc, m, l = vals
        return acc / l, m + jnp.log(l)

    def stash(g, outs, vals):
        o, lse = normalise(vals)
        for row0, n, dst, stride in outs:
            idx = pl.ds(dst, n, stride=stride)
            out_s[g - 1, idx, :] = o[row0:row0 + n]
            lse_s[g - 1, idx, :] = lse[row0:row0 + n]

    def merge(g, outs, vals):
        assert g == 0
        o0, lse0 = normalise(vals)
        for row0, n, dst, stride in outs:
            assert stride == 1
            sl = slice(dst, dst + n)
            os_ = [o0[row0:row0 + n], out_s[0, sl, :], out_s[1, sl, :]]
            ls_ = [lse0[row0:row0 + n], lse_s[0, sl, :], lse_s[1, sl, :]]
            mx = jnp.maximum(jnp.maximum(ls_[0], ls_[1]), ls_[2])
            ws = [jnp.exp(x - mx) for x in ls_]
            num = ws[0] * os_[0] + ws[1] * os_[1] + ws[2] * os_[2]
            o_ref[sl, :] = (num / (ws[0] + ws[1] + ws[2])).astype(BF16)

    def add(g, q_slices, keys_fn, v_slices, bias, outs):
        def score_fn():
            q = jnp.concatenate([q_refs[g][sl, :] for sl in q_slices], axis=0)
            return _split_scores(q, keys_fn(), bias)

        def finish_fn(scores):
            v = jnp.concatenate([v_refs[g][sl, :] for sl in v_slices], axis=0)
            (merge if g == 0 else stash)(g, outs, _pair_finish(scores, v))

        blocks.append((score_fn, finish_fn))

    def class_blocks(g, kt_ref, base, nat, stride):
        nc = len(base)
        rows = lambda j, lo, n: slice(base[j] + lo, base[j] + lo + n)
        add(g, [rows(0, 0, BAND)], lambda: kt_ref[:, rows(0, 0, CHUNK)], [rows(0, 0, CHUNK)],
            bias_first, [(0, BAND, nat[0], stride)])
        for j in range(nc - 1):
            add(g, [rows(j, BAND, BAND), rows(j + 1, 0, BAND)],
                lambda j=j: jnp.concatenate([kt_ref[:, rows(j, 0, CHUNK)], kt_ref[:, rows(j + 1, 0, CHUNK)]], axis=1),
                [rows(j, 0, CHUNK), rows(j + 1, 0, CHUNK)],
                bias_full, [(0, BAND, nat[j] + BAND * stride, stride), (BAND, BAND, nat[j + 1], stride)])
        add(g, [rows(nc - 1, BAND, BAND)], lambda: kt_ref[:, rows(nc - 1, 0, CHUNK)],
            [rows(nc - 1, 0, CHUNK)], bias_last, [(0, BAND, nat[nc - 1] + BAND * stride, stride)])

    nt = SEQ // TM
    d2 = DILATIONS[2]
    piece = TM // d2
    for r in range(d2):
        pieces = [slice(j * TM + r * piece, j * TM + (r + 1) * piece) for j in range(nt)]
        add(2, pieces,
            lambda pieces=pieces: [jnp.concatenate([k_ref[sl, :] for sl in pieces], axis=0) for k_ref in (k2a_ref, k2b_ref)],
            pieces, bias_one, [(j * piece, piece, j * TM + r, d2) for j in range(nt)])

    d1 = DILATIONS[1]
    for r in range(d1):
        class_blocks(1, kt1_ref, [j * TM + r * CHUNK for j in range(nt)], [j * TM + r for j in range(nt)], d1)

    chunk_rows = [c * CHUNK for c in range(SEQ // CHUNK)]
    class_blocks(0, kt0_ref, chunk_rows, chunk_rows, 1)


def _dilated(a0, kt0, a1, kt1, a2):
    b = a0.shape[0]
    n_pairs = GROUP_W // LANES
    slabs = lambda n: pl.BlockSpec((None, n, SEQ, LANES), lambda i: (i, 0, 0, 0))
    ktspec = pl.BlockSpec((None, GROUP_W, SEQ), lambda i: (i, 0, 0))
    return pl.pallas_call(
        _dilated_kernel,
        grid=(b,),
        in_specs=[slabs(2 * n_pairs), ktspec, slabs(2 * n_pairs), ktspec, slabs(4 * n_pairs)],
        out_specs=slabs(n_pairs),
        out_shape=jax.ShapeDtypeStruct((b, n_pairs, SEQ, LANES), BF16),
        scratch_shapes=[pltpu.VMEM((n_pairs, 2, SEQ, LANES), F32), pltpu.VMEM((n_pairs, 2, SEQ, LANES), F32)],
        compiler_params=pltpu.CompilerParams(dimension_semantics=("parallel",),
                                             vmem_limit_bytes=VMEM_LIMIT_BYTES),
        name="dilated",
    )(a0, kt0, a1, kt1, a2)


def _na_window_start(i):
    return int(np.clip(i - NA_ROWS // 2, 0, GRID_ROWS - NA_ROWS))


def _na_kernel(nb_ref, kt_full, bias_full, o_full, kts_full):
    blocks = []
    for p in range(NA_PAIRS_PER_STEP):
        _na_pair(blocks, nb_ref.at[0, p], nb_ref.at[1, p], nb_ref.at[2, p],
                 kt_full.at[p * LANES:(p + 1) * LANES], bias_full.at[p], o_full.at[p], kts_full.at[p])
    _software_pipeline(blocks, NA_PIPE_DEPTH)


def _na_pair(blocks, qa_ref, qb_ref, v_ref, kt_ref, bias_ref, o_ref, kts_ref):
    words = pltpu.bitcast(kt_ref[...], jnp.uint32)
    kts_ref[...] = pltpu.bitcast(pltpu.roll(words, SEQ - GRID_W, 1), BF16)

    for i in range(GRID_ROWS):
        lo = _na_window_start(i)
        case = NA_CASE_ROWS.index(i) if i in NA_CASE_ROWS else NA_CASE_ROWS.index(NA_ROWS // 2)
        qsl = slice(i * GRID_W, (i + 1) * GRID_W)
        vsl = slice(lo * GRID_W, lo * GRID_W + NA_KEYS)
        if lo % 2 == 0:
            keys_fn = lambda lo=lo: kt_ref[:, lo * GRID_W:lo * GRID_W + NA_KEYS]
        else:
            keys_fn = lambda lo=lo: kts_ref[:, (lo - 1) * GRID_W:(lo - 1) * GRID_W + NA_KEYS]

        def score_fn(qsl=qsl, keys_fn=keys_fn, case=case):
            q_stack = jnp.concatenate([qa_ref[qsl, :], qb_ref[qsl, :]], axis=0)
            return _stacked_scores(q_stack, keys_fn(), bias_ref[case])

        def finish_fn(scores, qsl=qsl, vsl=vsl):
            acc, _, l = _pair_finish(scores, v_ref[vsl, :])
            o_ref[qsl, :] = (acc / l).astype(BF16)

        blocks.append((score_fn, finish_fn))


def _nbr(layer, nb, ktb, bias_tab):
    b = nb.shape[0]
    n_pairs = NA_W // LANES
    pps = NA_PAIRS_PER_STEP
    nb_view = nb.reshape(b, 3, n_pairs // pps, pps, SEQ, LANES)
    return pl.pallas_call(
        _na_kernel,
        grid=(n_pairs // pps, b),
        in_specs=[pl.BlockSpec((None, 3, None, pps, SEQ, LANES), lambda g, i: (i, 0, g, 0, 0, 0)),
                  pl.BlockSpec((None, pps * LANES, SEQ), lambda g, i: (i, g, 0)),
                  pl.BlockSpec((None, pps, len(NA_CASE_ROWS), 2 * GRID_W, NA_KEYS), lambda g, i: (layer, g, 0, 0, 0))],
        out_specs=pl.BlockSpec((None, pps, SEQ, LANES), lambda g, i: (i, g, 0, 0)),
        out_shape=jax.ShapeDtypeStruct((b, n_pairs, SEQ, LANES), BF16),
        scratch_shapes=[pltpu.VMEM((pps, LANES, SEQ), BF16)],
        compiler_params=pltpu.CompilerParams(dimension_semantics=("parallel", "parallel"),
                                             vmem_limit_bytes=VMEM_LIMIT_BYTES),
        name="nbr",
    )(nb_view, ktb, bias_tab)


def _bias_table_kernel(rb_ref, o_ref):
    n = GRID_W
    qc = lax.broadcasted_iota(jnp.int32, (n, LANES), 0)
    lane = lax.broadcasted_iota(jnp.int32, (n, LANES), 1)
    kc = lane % n
    win_lo = jnp.clip(qc - NA_COLS // 2, 0, n - NA_COLS)
    windowed = lambda t: jnp.where(kc >= win_lo, jnp.where(kc < win_lo + NA_COLS, t, NEG_INF), NEG_INF)
    low = lane < n
    placed = []
    for i in range(2 * NA_ROWS - 1):
        v = jnp.broadcast_to(rb_ref[i:i + 1, :], (n, LANES))
        even = pltpu.roll(v, n + 1, 1, stride=1, stride_axis=0)
        odd = pltpu.roll(v, 1, 1, stride=1, stride_axis=0)
        placed.append((windowed(even), windowed(odd)))
    for c, i in enumerate(NA_CASE_ROWS):
        first = _na_window_start(i) - i + NA_ROWS - 1
        for t in range(NA_ROWS // 2):
            o_ref[c, :, t * LANES:(t + 1) * LANES] = jnp.where(low, placed[first + 2 * t][0],
                                                               placed[first + 2 * t + 1][1])


def _na_bias_table(rel_bias):
    nl = rel_bias.shape[0]
    n_dr = 2 * NA_ROWS - 1
    nc = len(NA_CASE_ROWS)
    pad_lo = (GRID_W - 1) - (NA_COLS - 1)
    rows = jnp.pad(rel_bias.astype(F32), ((0, 0), (0, 0), (0, 0), (pad_lo, LANES - (2 * NA_COLS - 1) - pad_lo)))
    return pl.pallas_call(
        _bias_table_kernel,
        grid=(nl, NA_HEADS),
        in_specs=[pl.BlockSpec((None, None, n_dr, LANES), lambda l, h: (l, h, 0, 0))],
        out_specs=pl.BlockSpec((None, None, nc, GRID_W, NA_KEYS), lambda l, h: (l, h // 2, 0, h % 2, 0)),
        out_shape=jax.ShapeDtypeStruct((nl, NA_HEADS // 2, nc, 2 * GRID_W, NA_KEYS), F32),
        compiler_params=pltpu.CompilerParams(dimension_semantics=("parallel", "parallel")),
        name="bias_table",
    )(rows)


def _out_ffn_kernel(x1_ref, ya_ref, yb_ref, gm_ref, wgate_ref, wa_ref, wb_ref, wout_ref,
                    g3_ref, wup_ref, wdown_ref, gf_ref, o_ref, *, final):
    n_split = TM_OUT // SUB_ROWS
    parts = [slice(i * SUB_ROWS, (i + 1) * SUB_ROWS) for i in range(n_split)]
    x1s, merged = [], []
    for r in parts:
        x1 = x1_ref[r, :]
        h = _rms(x1, gm_ref[...]).astype(BF16)
        gates = _sigmoid(jnp.dot(h, wgate_ref[...], preferred_element_type=F32))
        ya = jnp.concatenate([ya_ref[c, r, :] for c in range(ya_ref.shape[0])], axis=1)
        yb = jnp.concatenate([yb_ref[c, r, :] for c in range(yb_ref.shape[0])], axis=1)
        ba = jnp.dot(ya, wa_ref[...], preferred_element_type=F32)
        bb = jnp.dot(yb, wb_ref[...], preferred_element_type=F32)
        x1s.append(x1)
        merged.append((gates[:, :D_MODEL] * ba + gates[:, D_MODEL:] * bb).astype(BF16))
    x2s = [x1 + jnp.dot(m, wout_ref[...], preferred_element_type=F32) for x1, m in zip(x1s, merged)]
    h3s = [_rms(x2, g3_ref[...]).astype(BF16) for x2 in x2s]
    for r, x2, h3 in zip(parts, x2s, h3s):
        x3 = x2 + 0.5 * _swiglu_half(h3, wup_ref, wdown_ref)
        if final:
            x3 = _rms(x3, gf_ref[...])
        o_ref[r, :] = x3


def _out_ffn(layer, x1, ya, yb, gm, wgate, wa, wb, wout, g3, wup, wdown, gf, final):
    b = x1.shape[0]
    row = pl.BlockSpec((None, TM_OUT, D_MODEL), lambda i, j: (i, j, 0))
    slabs = lambda n: pl.BlockSpec((None, n, TM_OUT, LANES), lambda i, j: (i, 0, j, 0))
    weights = (gm, wgate, wa, wb, wout, g3, wup, wdown)
    gf_spec = pl.BlockSpec(gf.shape, lambda i, j: (0, 0))
    return pl.pallas_call(
        functools.partial(_out_ffn_kernel, final=final),
        grid=(b, SEQ // TM_OUT),
        in_specs=[row, slabs(ya.shape[1]), slabs(yb.shape[1])] + [_resident(w.shape, layer) for w in weights] + [gf_spec],
        out_specs=row,
        out_shape=jax.ShapeDtypeStruct((b, SEQ, D_MODEL), F32),
        compiler_params=pltpu.CompilerParams(dimension_semantics=("parallel", "parallel"),
                                             vmem_limit_bytes=VMEM_LIMIT_BYTES),
        name="out_ffn",
    )(x1, ya, yb, *weights, gf)


def _rope_tables():
    half = HEAD_DIM // 2
    pos = jnp.arange(SEQ)
    inv_freq = ROPE_THETA ** (-jnp.arange(half, dtype=F32) / half)
    ang = pos.astype(F32)[:, None] * inv_freq[None, :]
    cos = jnp.cos(ang)
    sin = jnp.sin(ang)
    cos_slab = jnp.tile(cos, (1, LANES // half))
    sin_slab = jnp.tile(jnp.concatenate([-sin, sin], axis=-1), (1, LANES // HEAD_DIM))

    def class_major(t, d):
        w = t.shape[-1]
        return t.reshape(SEQ // TM, TM // d, d, w).transpose(0, 2, 1, 3).reshape(SEQ, w)

    slabs = [cos_slab, sin_slab]
    for d in DILATIONS[1:]:
        slabs += [class_major(cos_slab, d), class_major(sin_slab, d)]
    d1 = DILATIONS[1]
    rows = [cos.T, sin.T, class_major(cos, d1).T, class_major(sin, d1).T]
    return jnp.concatenate(slabs, axis=1), jnp.concatenate(rows, axis=0)


def kernel(x, ffn1_norm, ffn1_w_up, ffn1_w_down, mix_norm, w_in, na_rel_bias, w_branch_a, w_branch_b,
           w_out, ffn2_norm, ffn2_w_up, ffn2_w_down, final_norm):
    assert x.shape[1:] == (SEQ, D_MODEL) and x.dtype == F32
    tab, tab_t = _rope_tables()
    gain = lambda g: g.astype(F32).reshape(-1, 1, D_MODEL)
    w = w_in.astype(BF16)
    aq, ak, av = (w[:, :, i * 3 * GROUP_W:(i + 1) * 3 * GROUP_W] for i in range(3))
    bq, bk, bv = (w[:, :, DIL_QKV + i * NA_W:DIL_QKV + (i + 1) * NA_W] for i in range(3))
    grp = lambda t, g: t[:, :, g * GROUP_W:(g + 1) * GROUP_W]
    wnat = jnp.concatenate([grp(aq, 0), grp(av, 0), bq, bv], axis=2)
    wknat = jnp.concatenate([grp(ak, 0), bk], axis=2).transpose(0, 2, 1)
    wg1 = jnp.concatenate([grp(aq, 1), grp(av, 1)], axis=2)
    wkg1 = grp(ak, 1).transpose(0, 2, 1)
    wg2 = jnp.concatenate([grp(aq, 2), grp(ak, 2), grp(av, 2)], axis=2)
    wgate = w[:, :, DIL_QKV + NA_QKV:]
    g_ffn1, g_mix, g_ffn2 = gain(ffn1_norm), gain(mix_norm), gain(ffn2_norm)
    up1, down1 = ffn1_w_up.astype(BF16), ffn1_w_down.astype(BF16)
    up2, down2 = ffn2_w_up.astype(BF16), ffn2_w_down.astype(BF16)
    wa, wb, wo = w_branch_a.astype(BF16), w_branch_b.astype(BF16), w_out.astype(BF16)
    bias_tab = _na_bias_table(na_rel_bias)
    g_final = final_norm.astype(F32).reshape(1, D_MODEL)

    for layer in range(DEPTH):
        x1, a0, kt0, a1, kt1, a2, nb, ktb = _ffn_proj(
            layer, x, g_ffn1, up1, down1, g_mix, wnat, wknat, wg1, wkg1, wg2, tab, tab_t)
        ya = _dilated(a0, kt0, a1, kt1, a2)
        yb = _nbr(layer, nb, ktb, bias_tab)
        x = _out_ffn(layer, x1, ya, yb, g_mix, wgate, wa, wb, wo, g_ffn2, up2, down2, g_final,
                     final=(layer == DEPTH - 1))
    return x
```
